```python
import jax, jax.numpy as jnp
from jax import lax
import numpy as np

D_MODEL = 2048
BATCH = 4
SEQ = 8192
DEPTH = 1
DEC_BATCH = 32
DEC_SEQ = 16
PAST_LEN = 2048

CHUNK = 64
SGU_CHUNK = 128
SGU_GROUPS = 8
D_SGU = D_MODEL
SGU_GROUP_DIM = D_SGU // SGU_GROUPS
SSM_EXPAND = 2
D_SSM = SSM_EXPAND * D_MODEL
SSM_HEADDIM = 64
SSM_HEADS = D_SSM // SSM_HEADDIM
SSM_GROUPS = 8
SSM_HPG = SSM_HEADS // SSM_GROUPS
SSM_STATE = 128
CONV_WIDTH = 4
CONV_DIM = D_SSM + 2 * SSM_GROUPS * SSM_STATE
SSM_BLOCK = CHUNK
N_BRANCH = 2
D_IN = 2 * D_SGU + D_SSM + CONV_DIM + SSM_HEADS + N_BRANCH * D_MODEL
D_FF = ((8 * D_MODEL // 3 + 255) // 256) * 256
EPS = 1e-6

kernel_name = "hybrid_sgu_ssd_streaming_step"


def _rmsnorm(x, g):
    xf = x.astype(jnp.float32)
    y = xf * lax.rsqrt(jnp.mean(xf * xf, axis=-1, keepdims=True) + EPS)
    return (y * g.astype(jnp.float32)).astype(x.dtype)


def _layernorm(x, g, b):
    xf = x.astype(jnp.float32)
    mu = jnp.mean(xf, axis=-1, keepdims=True)
    var = jnp.mean(jnp.square(xf - mu), axis=-1, keepdims=True)
    y = (xf - mu) * lax.rsqrt(var + EPS)
    return (y * g.astype(jnp.float32) + b.astype(jnp.float32)).astype(x.dtype)


def _sgu_mix(v, w_s, b_s):
    bsz, seq = v.shape[:2]
    q = min(seq, SGU_CHUNK)
    mask = jnp.tril(jnp.ones((q, q), dtype=w_s.dtype))
    w = w_s[:, :q, :q] * mask
    vc = v.reshape(bsz, seq // q, q, SGU_GROUPS, SGU_GROUP_DIM)
    out = jnp.einsum('gts,bcsgd->bctgd', w, vc) + b_s[:, :q].T[None, None, :, :, None]
    return out.reshape(bsz, seq, D_SGU)


def _causal_conv(xbc, buf, w, b):
    seq = xbc.shape[1]
    xpad = jnp.concatenate([buf.astype(xbc.dtype), xbc], axis=1)
    out = b
    for k in range(CONV_WIDTH):
        out = out + xpad[:, k:k + seq] * w[k]
    return out, xpad[:, seq:]


def _ssd(x, dt, a, bmat, cmat, h0, block):
    f32 = jnp.float32
    bsz, seq = x.shape[:2]
    nc = seq // block
    xdt = (x.astype(f32) * dt[..., None]).reshape(bsz, nc, block, SSM_GROUPS, SSM_HPG, SSM_HEADDIM)
    da = (dt * a).reshape(bsz, nc, block, SSM_GROUPS, SSM_HPG)
    bm = bmat.astype(f32).reshape(bsz, nc, block, SSM_GROUPS, SSM_STATE)
    cm = cmat.astype(f32).reshape(bsz, nc, block, SSM_GROUPS, SSM_STATE)
    xs_in = tuple(jnp.moveaxis(t, 1, 0) for t in (xdt, da, bm, cm))
    mask = jnp.tril(jnp.ones((block, block), dtype=bool))[None, :, :, None, None]

    def step(h, inp):
        xc, dac, bc, cc = inp
        cum = jnp.cumsum(dac, axis=1)
        seg = cum[:, :, None] - cum[:, None, :]
        decay = jnp.exp(jnp.where(mask, seg, -jnp.inf))
        cb = jnp.einsum('btgn,bsgn->btsg', cc, bc)
        y_in = jnp.einsum('btsg,btsgh,bsghp->btghp', cb, decay, xc)
        y_st = jnp.einsum('btgn,bghpn->btghp', cc, h) * jnp.exp(cum)[..., None]
        tail = jnp.exp(cum[:, -1:] - cum)
        h_new = h * jnp.exp(cum[:, -1])[..., None, None] + jnp.einsum('bsgn,bsgh,bsghp->bghpn', bc, tail, xc)
        return h_new, y_in + y_st

    h0r = h0.astype(f32).reshape(bsz, SSM_GROUPS, SSM_HPG, SSM_HEADDIM, SSM_STATE)
    h_t, ys = lax.scan(step, h0r, xs_in)
    ys = jnp.moveaxis(ys, 0, 1).reshape(bsz, seq, SSM_HEADS, SSM_HEADDIM)
    return ys, h_t.reshape(bsz, SSM_HEADS, SSM_HEADDIM, SSM_STATE)


def _mixer(h, ssm_h0, conv_buf, w_in, ln_g, ln_b, sgu_w, sgu_b, w_a, conv_w, conv_b,
           dt_bias, a_log, d_skip, ssm_norm_g, w_b, w_o):
    bsz, seq = h.shape[:2]
    proj = h @ w_in
    s0 = 2 * D_SGU
    s1 = s0 + D_SSM
    s2 = s1 + CONV_DIM
    s3 = s2 + SSM_HEADS
    uv, z, xbc, dt_raw, gates = jnp.split(proj, [s0, s1, s2, s3], axis=-1)
    uv = jax.nn.gelu(uv)
    u, v = jnp.split(uv, 2, axis=-1)
    v = _layernorm(v, ln_g, ln_b)
    y_a = (u * _sgu_mix(v, sgu_w, sgu_b)) @ w_a
    xbc, new_buf = _causal_conv(xbc, conv_buf, conv_w, conv_b)
    xbc = jax.nn.silu(xbc)
    xs, bmat, cmat = jnp.split(xbc, [D_SSM, D_SSM + SSM_GROUPS * SSM_STATE], axis=-1)
    xs = xs.reshape(bsz, seq, SSM_HEADS, SSM_HEADDIM)
    bmat = bmat.reshape(bsz, seq, SSM_GROUPS, SSM_STATE)
    cmat = cmat.reshape(bsz, seq, SSM_GROUPS, SSM_STATE)
    dt = jax.nn.softplus(dt_raw.astype(jnp.float32) + dt_bias.astype(jnp.float32))
    a = -jnp.exp(a_log.astype(jnp.float32))
    y_ssd, new_h = _ssd(xs, dt, a, bmat, cmat, ssm_h0, min(SSM_BLOCK, seq))
    y_ssd = y_ssd + d_skip.astype(jnp.float32)[:, None] * xs.astype(jnp.float32)
    y = y_ssd.reshape(bsz, seq, D_SSM).astype(h.dtype) * jax.nn.silu(z)
    y = _rmsnorm(y.reshape(bsz, seq, SSM_GROUPS, D_SSM // SSM_GROUPS),
                 ssm_norm_g.reshape(SSM_GROUPS, D_SSM // SSM_GROUPS)).reshape(bsz, seq, D_SSM)
    y_b = y @ w_b
    g_a, g_b = jnp.split(jax.nn.sigmoid(gates), 2, axis=-1)
    out = (g_a * y_a + g_b * y_b) @ w_o
    return out, new_h.astype(ssm_h0.dtype), new_buf, v


def _layer(x, ssm_h0, conv_buf, norm_mix_g, w_in, ln_g, ln_b, sgu_w, sgu_b, w_a, conv_w, conv_b,
           dt_bias, a_log, d_skip, ssm_norm_g, w_b, w_o, norm_ffn_g, w_gate, w_up, w_down):
    mix, new_h, new_buf, v = _mixer(_rmsnorm(x, norm_mix_g), ssm_h0, conv_buf, w_in, ln_g, ln_b,
                                    sgu_w, sgu_b, w_a, conv_w, conv_b, dt_bias, a_log, d_skip,
                                    ssm_norm_g, w_b, w_o)
    x = x + mix
    hf = _rmsnorm(x, norm_ffn_g)
    x = x + (jax.nn.silu(hf @ w_gate) * (hf @ w_up)) @ w_down
    return x, new_h, new_buf, v


def setup_inputs(seed: int = 0) -> dict:
    key = jax.random.key(seed)
    ks = jax.random.split(key, 24)
    f32 = jnp.float32
    nrm = lambda k, shape, scale: jax.random.normal(k, shape, f32) * scale
    dt0 = jnp.exp(jax.random.uniform(ks[10], (DEPTH, SSM_HEADS), f32, np.log(1e-3), np.log(1e-1)))
    return {
        "x_prompt": nrm(ks[0], (BATCH, SEQ, D_MODEL), 1.0),
        "x_sample": nrm(ks[1], (DEC_BATCH, DEC_SEQ, D_MODEL), 1.0),
        "state_ssm": nrm(ks[2], (DEPTH, DEC_BATCH, SSM_HEADS, SSM_HEADDIM, SSM_STATE), 0.1),
        "state_conv": nrm(ks[3], (DEPTH, DEC_BATCH, CONV_WIDTH - 1, CONV_DIM), 1.0),
        "norm_mix_g": 1.0 + nrm(ks[4], (DEPTH, D_MODEL), 0.02),
        "w_in": nrm(ks[5], (DEPTH, D_MODEL, D_IN), D_MODEL ** -0.5),
        "sgu_ln_g": 1.0 + nrm(ks[6], (DEPTH, D_SGU), 0.02),
        "sgu_ln_b": nrm(ks[7], (DEPTH, D_SGU), 0.02),
        "sgu_w": nrm(ks[8], (DEPTH, SGU_GROUPS, SGU_CHUNK, SGU_CHUNK), SGU_CHUNK ** -0.5),
        "sgu_b": 1.0 + nrm(ks[9], (DEPTH, SGU_GROUPS, SGU_CHUNK), 0.02),
        "w_a": nrm(ks[11], (DEPTH, D_SGU, D_MODEL), D_SGU ** -0.5),
        "conv_w": nrm(ks[12], (DEPTH, CONV_WIDTH, CONV_DIM), CONV_WIDTH ** -0.5),
        "conv_b": nrm(ks[13], (DEPTH, CONV_DIM), 0.02),
        "dt_bias": dt0 + jnp.log(-jnp.expm1(-dt0)),
        "a_log": jnp.log(jax.random.uniform(ks[14], (DEPTH, SSM_HEADS), f32, 1.0, 16.0)),
        "d_skip": 1.0 + nrm(ks[15], (DEPTH, SSM_HEADS), 0.02),
        "ssm_norm_g": 1.0 + nrm(ks[16], (DEPTH, D_SSM), 0.02),
        "w_b": nrm(ks[17], (DEPTH, D_SSM, D_MODEL), D_SSM ** -0.5),
        "w_o": nrm(ks[18], (DEPTH, D_MODEL, D_MODEL), D_MODEL ** -0.5),
        "norm_ffn_g": 1.0 + nrm(ks[19], (DEPTH, D_MODEL), 0.02),
        "w_gate": nrm(ks[20], (DEPTH, D_MODEL, D_FF), D_MODEL ** -0.5),
        "w_up": nrm(ks[21], (DEPTH, D_MODEL, D_FF), D_MODEL ** -0.5),
        "w_down": nrm(ks[22], (DEPTH, D_FF, D_MODEL), D_FF ** -0.5),
        "norm_final_g": 1.0 + nrm(ks[23], (D_MODEL,), 0.02),
    }


def reference(x_prompt, x_sample, state_ssm, state_conv, norm_mix_g, w_in, sgu_ln_g, sgu_ln_b,
              sgu_w, sgu_b, w_a, conv_w, conv_b, dt_bias, a_log, d_skip, ssm_norm_g, w_b, w_o,
              norm_ffn_g, w_gate, w_up, w_down, norm_final_g):
    params = (norm_mix_g, w_in, sgu_ln_g, sgu_ln_b, sgu_w, sgu_b, w_a, conv_w, conv_b, dt_bias,
              a_log, d_skip, ssm_norm_g, w_b, w_o, norm_ffn_g, w_gate, w_up, w_down)
    xp, xs = x_prompt, x_sample
    ssm_p, conv_p, ssm_s, conv_s, v_s = [], [], [], [], []
    for i in range(DEPTH):
        lp = [p[i] for p in params]
        h0_p = jnp.zeros((x_prompt.shape[0], SSM_HEADS, SSM_HEADDIM, SSM_STATE), x_prompt.dtype)
        buf_p = jnp.zeros((x_prompt.shape[0], CONV_WIDTH - 1, CONV_DIM), x_prompt.dtype)
        xp, hp, bp, _ = _layer(xp, h0_p, buf_p, *lp)
        xs, hs, bs, vs = _layer(xs, state_ssm[i], state_conv[i], *lp)
        ssm_p.append(hp)
        conv_p.append(bp)
        ssm_s.append(hs)
        conv_s.append(bs)
        v_s.append(vs)
    y_prompt = _rmsnorm(xp, norm_final_g)
    y_sample = _rmsnorm(xs, norm_final_g)
    new_ssm_prompt = jnp.stack(ssm_p)
    new_conv_prompt = jnp.stack(conv_p)
    new_ssm_sample = jnp.stack(ssm_s)
    new_conv_sample = jnp.stack(conv_s)
    new_sgu_v_sample = jnp.stack(v_s)
    return (y_prompt, y_sample, new_ssm_prompt, new_conv_prompt, new_ssm_sample, new_conv_sample, new_sgu_v_sample)
```

```python
import functools

import jax
import jax.numpy as jnp
from jax import lax
from jax.experimental import pallas as pl
from jax.experimental.pallas import tpu as pltpu

F32 = jnp.float32
BF16 = jnp.bfloat16

EPS = 1e-6
SGU_CHUNK = 128
SGU_GROUPS = 8
SSM_HEADDIM = 64
SSM_GROUPS = 8
SSM_STATE = 128
CONV_WIDTH = 4
SSD_CHUNK = 64

V7X_VMEM_LIMIT_BYTES = 56 * 1024 * 1024
LANES = 128
CONV_PAD_ROWS = 8


def _cparams(*sem):
    return pltpu.CompilerParams(dimension_semantics=sem, vmem_limit_bytes=V7X_VMEM_LIMIT_BYTES)


def _dot(a, b):
    return jnp.dot(a, b, preferred_element_type=F32)


def _dot_nt(a, b):
    return lax.dot_general(a, b, (((1,), (1,)), ((), ())), preferred_element_type=F32)


def _rms(x, g):
    return x * lax.rsqrt(jnp.mean(x * x, axis=-1, keepdims=True) + EPS) * g


def _split_bf16(x):
    hi = x.astype(BF16)
    lo = (x - hi.astype(F32)).astype(BF16)
    return hi, lo


def _act_gelu(x):
    return jax.nn.gelu(x, approximate=True)


def _act_silu(x):
    return x * jax.nn.sigmoid(x)


def _act_softplus(x):
    return jnp.maximum(x, 0.0) + jnp.log1p(jnp.exp(-jnp.abs(x)))


def _act_none(x):
    return x


def _inproj_kernel(x_ref, g_ref, w_ref, b_ref, o_ref, xn_ref, *, act):
    @pl.when(pl.program_id(1) == 0)
    def _():
        xn_ref[...] = _rms(x_ref[...], g_ref[...]).astype(BF16)

    acc = _dot(xn_ref[...], w_ref[...]) + b_ref[...]
    o_ref[...] = act(acc).astype(o_ref.dtype)


def _inproj(x, g, w, b, act, out_dtype, tm, tn):
    rows, d = x.shape
    n = w.shape[1]
    return pl.pallas_call(
        functools.partial(_inproj_kernel, act=act),
        grid=(rows // tm, n // tn),
        in_specs=[
            pl.BlockSpec((tm, d), lambda i, j: (i, 0)),
            pl.BlockSpec((1, d), lambda i, j: (0, 0)),
            pl.BlockSpec((d, tn), lambda i, j: (0, j)),
            pl.BlockSpec((1, tn), lambda i, j: (0, j)),
        ],
        out_specs=pl.BlockSpec((tm, tn), lambda i, j: (i, j)),
        out_shape=jax.ShapeDtypeStruct((rows, n), out_dtype),
        scratch_shapes=[pltpu.VMEM((tm, d), BF16)],
        compiler_params=_cparams("arbitrary", "arbitrary"),
        name="inproj",
    )(x, g, w, b)


def _sgu_kernel(uv_ref, ga_ref, lng_ref, lnb_ref, wm_ref, bias_ref, wa_ref, o_ref, v_ref,
                gated_ref, *, seg_len):
    tm = uv_ref.shape[0]
    d = o_ref.shape[1]
    gd = d // SGU_GROUPS
    v = uv_ref[:, d:].astype(F32)
    mu = jnp.mean(v, axis=-1, keepdims=True)
    vc = v - mu
    var = jnp.mean(vc * vc, axis=-1, keepdims=True)
    vn = vc * lax.rsqrt(var + EPS) * lng_ref[...] + lnb_ref[...]
    v_ref[...] = vn
    vb = vn.astype(BF16)
    row = lax.broadcasted_iota(jnp.int32, (SGU_CHUNK, SGU_CHUNK), 0)
    col = lax.broadcasted_iota(jnp.int32, (SGU_CHUNK, SGU_CHUNK), 1)
    mask = (col <= row) & ((row // seg_len) == (col // seg_len))
    for g in range(SGU_GROUPS):
        w = jnp.where(mask, wm_ref[g], 0.0).astype(BF16)
        cs = slice(g * gd, (g + 1) * gd)
        for c in range(tm // SGU_CHUNK):
            rs = slice(c * SGU_CHUNK, (c + 1) * SGU_CHUNK)
            mixed = _dot(w, vb[rs, cs]) + bias_ref[:, cs]
            gated_ref[rs, cs] = (uv_ref[rs, cs].astype(F32) * mixed).astype(BF16)
    ya = _dot(gated_ref[...], wa_ref[...])
    o_ref[...] = (ga_ref[...].astype(F32) * ya).astype(o_ref.dtype)


def _sgu(uv, gates, ln_g, ln_b, wm, bias_full, w_a, seg_len, tm):
    rows = uv.shape[0]
    d = w_a.shape[0]
    return pl.pallas_call(
        functools.partial(_sgu_kernel, seg_len=seg_len),
        grid=(rows // tm,),
        in_specs=[
            pl.BlockSpec((tm, 2 * d), lambda i: (i, 0)),
            pl.BlockSpec((tm, d), lambda i: (i, 0)),
            pl.BlockSpec((1, d), lambda i: (0, 0)),
            pl.BlockSpec((1, d), lambda i: (0, 0)),
            pl.BlockSpec((SGU_GROUPS, SGU_CHUNK, SGU_CHUNK), lambda i: (0, 0, 0)),
            pl.BlockSpec((SGU_CHUNK, d), lambda i: (0, 0)),
            pl.BlockSpec((d, d), lambda i: (0, 0)),
        ],
        out_specs=[
            pl.BlockSpec((tm, d), lambda i: (i, 0)),
            pl.BlockSpec((tm, d), lambda i: (i, 0)),
        ],
        out_shape=[
            jax.ShapeDtypeStruct((rows, d), BF16),
            jax.ShapeDtypeStruct((rows, d), F32),
        ],
        scratch_shapes=[pltpu.VMEM((tm, d), BF16)],
        compiler_params=_cparams("arbitrary"),
        name="sgu",
    )(uv, gates, ln_g, ln_b, wm, bias_full, w_a)


def _ssd_kernel(xbc_ref, dt_ref, z_ref, cw_ref, cb_ref, a_ref, d_ref, ng_ref, h0_ref, c0_ref,
                y_ref, hout_ref, cout_ref, ht_ref, cbuf_ref, *, n_tiles, valid_last):
    t = pl.program_id(1)
    q = SSD_CHUNK
    d_ssm = y_ref.shape[1]
    n_heads = d_ssm // SSM_HEADDIM
    gw = d_ssm // SSM_GROUPS
    pair = 2 * SSM_HEADDIM
    assert pair == LANES and q == SSM_HEADDIM

    @pl.when(t == 0)
    def _init():
        for g in range(SSM_GROUPS):
            ht_ref[g] = h0_ref[0, g].T
        cbuf_ref[0:CONV_PAD_ROWS, :] = jnp.zeros((CONV_PAD_ROWS, cbuf_ref.shape[1]), F32)
        cbuf_ref[CONV_PAD_ROWS - (CONV_WIDTH - 1):CONV_PAD_ROWS, :] = c0_ref[0]

    cbuf_ref[CONV_PAD_ROWS:CONV_PAD_ROWS + q, :] = xbc_ref[...].astype(F32)
    conv = cb_ref[...]
    for k in range(CONV_WIDTH):
        r0 = CONV_PAD_ROWS - (CONV_WIDTH - 1) + k
        conv = conv + cbuf_ref[r0:r0 + q, :] * cw_ref[k:k + 1, :]

    @pl.when(t == n_tiles - 1)
    def _conv_out():
        r0 = CONV_PAD_ROWS + valid_last - (CONV_WIDTH - 1)
        cout_ref[0] = cbuf_ref[r0:r0 + CONV_WIDTH - 1, :]

    cbuf_ref[0:CONV_PAD_ROWS, :] = cbuf_ref[q:q + CONV_PAD_ROWS, :]
    xc = conv * jax.nn.sigmoid(conv)
    xs = xc[:, :d_ssm]
    xs_bf = xs.astype(BF16)
    bm = xc[:, d_ssm:d_ssm + SSM_GROUPS * SSM_STATE].astype(BF16)
    cm = xc[:, d_ssm + SSM_GROUPS * SSM_STATE:].astype(BF16)

    dt = dt_ref[...]
    da = dt * a_ref[...]
    lane = lax.broadcasted_iota(jnp.int32, (q, d_ssm), 1)
    row = lax.broadcasted_iota(jnp.int32, (q, d_ssm), 0)
    pos = lane & (SSM_HEADDIM - 1)
    hrow = lax.broadcasted_iota(jnp.int32, (n_heads, d_ssm), 0)
    hlane = lax.broadcasted_iota(jnp.int32, (n_heads, d_ssm), 1)
    expand = (hrow == (hlane // SSM_HEADDIM)).astype(BF16)
    r2 = lax.broadcasted_iota(jnp.int32, (q, q), 0)
    c2 = lax.broadcasted_iota(jnp.int32, (q, q), 1)
    lower = (c2 <= r2).astype(BF16)
    ones = jnp.ones((q, q), BF16)

    da_hi, da_lo = _split_bf16(da)
    dab_hi = _dot(da_hi, expand)
    dab_lo = _dot(da_lo, expand)
    colb = _dot(lower, dab_hi.astype(BF16)) + _dot(lower, dab_lo.astype(BF16))
    upto = row <= pos
    rowb = (_dot(ones, jnp.where(upto, dab_hi, 0.0).astype(BF16))
            + _dot(ones, jnp.where(upto, dab_lo, 0.0).astype(BF16)))
    dtb = _dot(dt.astype(BF16), expand)
    dtrow = _dot(ones, jnp.where(row == pos, dtb, 0.0).astype(BF16))
    seg = colb - rowb
    mw = jnp.where(row >= pos, jnp.exp(jnp.minimum(seg, 0.0)), 0.0) * dtrow
    expc = jnp.exp(colb)
    decl = expc[q - 1:q, :]
    tailw = jnp.exp(jnp.minimum(colb[q - 1:q, :] - rowb[0:1, :], 0.0)) * dtrow[0:1, :]

    eye = (lax.broadcasted_iota(jnp.int32, (SSM_STATE, SSM_STATE), 0)
           == lax.broadcasted_iota(jnp.int32, (SSM_STATE, SSM_STATE), 1)).astype(BF16)
    first_head = lax.broadcasted_iota(jnp.int32, (q, pair), 1) < SSM_HEADDIM

    for g in range(SSM_GROUPS):
        ns = slice(g * SSM_STATE, (g + 1) * SSM_STATE)
        bg = bm[:, ns]
        cg = cm[:, ns]
        bb = jnp.concatenate([bg, bg], axis=0)
        cb2 = _dot_nt(cg, bb)
        bt2 = _dot_nt(eye, bb)
        htg = ht_ref[g]
        yst = _dot(cg, htg.astype(BF16))
        ys = []
        for j in range(gw // pair):
            cs = slice(g * gw + j * pair, g * gw + (j + 1) * pair)
            js = slice(j * pair, (j + 1) * pair)
            m2 = (cb2 * mw[:, cs]).astype(BF16)
            xs2 = xs_bf[:, cs]
            zero = jnp.zeros_like(xs2)
            xbd = jnp.concatenate([jnp.where(first_head, xs2, zero),
                                   jnp.where(first_head, zero, xs2)], axis=0)
            y = _dot(m2, xbd) + expc[:, cs] * yst[:, js] + d_ref[:, cs] * xs[:, cs]
            btw = (bt2 * tailw[:, cs]).astype(BF16)
            ht_ref[g, :, js] = htg[:, js] * decl[:, cs] + _dot(btw, xbd)
            ys.append(y)
        gs = slice(g * gw, (g + 1) * gw)
        yg = jnp.concatenate(ys, axis=1) * z_ref[:, gs].astype(F32)
        y_ref[:, gs] = _rms(yg, ng_ref[:, gs]).astype(y_ref.dtype)

    @pl.when(t == n_tiles - 1)
    def _state_out():
        for g in range(SSM_GROUPS):
            hout_ref[0, g] = ht_ref[g].T


def _ssd(xbc, dt, zs, conv_w, conv_b, a_row, d_full, norm_g, h0, c0, n_streams, n_tiles,
         valid_last):
    rows, conv_dim = xbc.shape
    d_ssm = zs.shape[1]
    n_heads = dt.shape[1]
    gw = d_ssm // SSM_GROUPS
    q = SSD_CHUNK
    rb = lambda s, t: (s * n_tiles + t, 0)
    const2 = lambda s, t: (0, 0)
    return pl.pallas_call(
        functools.partial(_ssd_kernel, n_tiles=n_tiles, valid_last=valid_last),
        grid=(n_streams, n_tiles),
        in_specs=[
            pl.BlockSpec((q, conv_dim), rb),
            pl.BlockSpec((q, n_heads), rb),
            pl.BlockSpec((q, d_ssm), rb),
            pl.BlockSpec((CONV_WIDTH, conv_dim), const2),
            pl.BlockSpec((1, conv_dim), const2),
            pl.BlockSpec((1, n_heads), const2),
            pl.BlockSpec((1, d_ssm), const2),
            pl.BlockSpec((1, d_ssm), const2),
            pl.BlockSpec((1, SSM_GROUPS, gw, SSM_STATE), lambda s, t: (s, 0, 0, 0)),
            pl.BlockSpec((1, CONV_WIDTH - 1, conv_dim), lambda s, t: (s, 0, 0)),
        ],
        out_specs=[
            pl.BlockSpec((q, d_ssm), rb),
            pl.BlockSpec((1, SSM_GROUPS, gw, SSM_STATE), lambda s, t: (s, 0, 0, 0)),
            pl.BlockSpec((1, CONV_WIDTH - 1, conv_dim), lambda s, t: (s, 0, 0)),
        ],
        out_shape=[
            jax.ShapeDtypeStruct((rows, d_ssm), BF16),
            jax.ShapeDtypeStruct((n_streams, SSM_GROUPS, gw, SSM_STATE), F32),
            jax.ShapeDtypeStruct((n_streams, CONV_WIDTH - 1, conv_dim), F32),
        ],
        scratch_shapes=[
            pltpu.VMEM((SSM_GROUPS, SSM_STATE, gw), F32),
            pltpu.VMEM((CONV_PAD_ROWS + q, conv_dim), F32),
        ],
        compiler_params=_cparams("arbitrary", "arbitrary"),
        name="ssd",
    )(xbc, dt, zs, conv_w, conv_b, a_row, d_full, norm_g, h0, c0)


def _outproj_kernel(yn_ref, gaya_ref, gb_ref, x_ref, wb_ref, wo_ref, o_ref):
    yb = _dot(yn_ref[...], wb_ref[...])
    merged = gaya_ref[...].astype(F32) + gb_ref[...].astype(F32) * yb
    o_ref[...] = x_ref[...] + _dot(merged.astype(BF16), wo_ref[...])


def _outproj(yn, gaya, gates, x, w_b, w_o, tm):
    rows, d = x.shape
    d_ssm = yn.shape[1]
    return pl.pallas_call(
        _outproj_kernel,
        grid=(rows // tm,),
        in_specs=[
            pl.BlockSpec((tm, d_ssm), lambda i: (i, 0)),
            pl.BlockSpec((tm, d), lambda i: (i, 0)),
            pl.BlockSpec((tm, d), lambda i: (i, 1)),
            pl.BlockSpec((tm, d), lambda i: (i, 0)),
            pl.BlockSpec((d_ssm, d), lambda i: (0, 0), pipeline_mode=pl.Buffered(1)),
            pl.BlockSpec((d, d), lambda i: (0, 0), pipeline_mode=pl.Buffered(1)),
        ],
        out_specs=pl.BlockSpec((tm, d), lambda i: (i, 0)),
        out_shape=jax.ShapeDtypeStruct((rows, d), F32),
        compiler_params=_cparams("arbitrary"),
        name="outproj",
    )(yn, gaya, gates, x, w_b, w_o)


def _ffn_kernel(x_ref, g_ref, wg_ref, wu_ref, wd_ref, gf_ref, o_ref, hf_ref, acc_ref, *, n_ff):
    j = pl.program_id(1)

    @pl.when(j == 0)
    def _():
        hf_ref[...] = _rms(x_ref[...], g_ref[...]).astype(BF16)
        acc_ref[...] = jnp.zeros_like(acc_ref)

    hf = hf_ref[...]
    gate = _dot(hf, wg_ref[...])
    up = _dot(hf, wu_ref[...])
    mid = (gate * jax.nn.sigmoid(gate) * up).astype(BF16)
    acc_ref[...] += _dot(mid, wd_ref[...])

    @pl.when(j == n_ff - 1)
    def _():
        o_ref[...] = _rms(x_ref[...] + acc_ref[...], gf_ref[...])


def _ffn(x, g, w_gate, w_up, w_down, g_final, tm, tf):
    rows, d = x.shape
    d_ff = w_gate.shape[1]
    n_ff = d_ff // tf
    return pl.pallas_call(
        functools.partial(_ffn_kernel, n_ff=n_ff),
        grid=(rows // tm, n_ff),
        in_specs=[
            pl.BlockSpec((tm, d), lambda i, j: (i, 0)),
            pl.BlockSpec((1, d), lambda i, j: (0, 0)),
            pl.BlockSpec((d, tf), lambda i, j: (0, j)),
            pl.BlockSpec((d, tf), lambda i, j: (0, j)),
            pl.BlockSpec((tf, d), lambda i, j: (j, 0)),
            pl.BlockSpec((1, d), lambda i, j: (0, 0)),
        ],
        out_specs=pl.BlockSpec((tm, d), lambda i, j: (i, 0)),
        out_shape=jax.ShapeDtypeStruct((rows, d), F32),
        scratch_shapes=[pltpu.VMEM((tm, d), BF16), pltpu.VMEM((tm, d), F32)],
        compiler_params=_cparams("arbitrary", "arbitrary"),
        name="ffn",
    )(x, g, w_gate, w_up, w_down, g_final)


def _pick_tile(rows, preferred):
    tile = min(rows, preferred)
    assert rows % tile == 0, (rows, tile)
    return tile


def _layer(x, h0, c0, p, g_final):
    n_streams, length, d = x.shape
    rows = n_streams * length
    x2 = x.reshape(rows, d)
    d_ssm = p["w_b"].shape[0]
    n_heads = p["dt_bias"].shape[0]
    conv_dim = p["conv_w"].shape[1]
    zero_bias = lambda n: jnp.zeros((1, n), F32)
    g_mix = p["norm_mix_g"].reshape(1, d)

    tm = _pick_tile(rows, 1024)
    proj = functools.partial(_inproj, x2, g_mix, tm=tm)
    uv = proj(p["w_uv"], zero_bias(2 * d), _act_gelu, BF16, tn=1024)
    zs = proj(p["w_z"], zero_bias(d_ssm), _act_silu, BF16, tn=1024)
    xbc = proj(p["w_xbc"], zero_bias(conv_dim), _act_none, BF16, tn=1024)
    dt = proj(p["w_dt"], p["dt_bias"].reshape(1, n_heads), _act_softplus, F32, tn=n_heads)
    gates = proj(p["w_gates"], zero_bias(2 * d), jax.nn.sigmoid, BF16, tn=1024)

    seg_len = min(length, SGU_CHUNK)
    assert SGU_CHUNK % seg_len == 0 and length % seg_len == 0
    rep = SGU_CHUNK // seg_len
    wm = jnp.tile(p["sgu_w"][:, :seg_len, :seg_len], (1, rep, rep))
    bias_full = jnp.repeat(jnp.tile(p["sgu_b"][:, :seg_len], (1, rep)).T, d // SGU_GROUPS, axis=1)
    gaya, v_norm = _sgu(uv, gates, p["sgu_ln_g"].reshape(1, d), p["sgu_ln_b"].reshape(1, d),
                        wm, bias_full, p["w_a"], seg_len, _pick_tile(rows, 512))

    q = SSD_CHUNK
    n_tiles = -(-length // q)
    padded = n_tiles * q
    valid_last = length - (n_tiles - 1) * q

    def pad_rows(arr):
        if padded == length:
            return arr
        arr = arr.reshape(n_streams, length, arr.shape[-1])
        arr = jnp.pad(arr, ((0, 0), (0, padded - length), (0, 0)))
        return arr.reshape(n_streams * padded, arr.shape[-1])

    gw = d_ssm // SSM_GROUPS
    a_row = (-jnp.exp(p["a_log"].astype(F32))).reshape(1, n_heads)
    d_full = jnp.repeat(p["d_skip"].astype(F32), SSM_HEADDIM).reshape(1, d_ssm)
    yn, h_new, c_new = _ssd(pad_rows(xbc), pad_rows(dt), pad_rows(zs), p["conv_w"],
                            p["conv_b"].reshape(1, conv_dim), a_row, d_full,
                            p["ssm_norm_g"].reshape(1, d_ssm),
                            h0.reshape(n_streams, SSM_GROUPS, gw, SSM_STATE), c0,
                            n_streams, n_tiles, valid_last)
    if padded != length:
        yn = yn.reshape(n_streams, padded, d_ssm)[:, :length].reshape(rows, d_ssm)
    h_new = h_new.reshape(n_streams, n_heads, SSM_HEADDIM, SSM_STATE)

    x1 = _outproj(yn, gaya, gates, x2, p["w_b"], p["w_o"], _pick_tile(rows, 256))
    y = _ffn(x1, p["norm_ffn_g"].reshape(1, d), p["w_gate"], p["w_up"], p["w_down"],
             g_final.reshape(1, d), _pick_tile(rows, 512), 512)
    return (y.reshape(n_streams, length, d), h_new, c_new,
            v_norm.reshape(n_streams, length, d))


def _layer_params(i, norm_mix_g, w_in, sgu_ln_g, sgu_ln_b, sgu_w, sgu_b, w_a, conv_w, conv_b,
                  dt_bias, a_log, d_skip, ssm_norm_g, w_b, w_o, norm_ffn_g, w_gate, w_up, w_down):
    d = w_in.shape[1]
    d_ssm = w_b.shape[1]
    conv_dim = conv_w.shape[2]
    n_heads = dt_bias.shape[1]
    s0 = 2 * d
    s1 = s0 + d_ssm
    s2 = s1 + conv_dim
    s3 = s2 + n_heads
    wi = w_in[i]
    return dict(
        norm_mix_g=norm_mix_g[i],
        w_uv=wi[:, :s0].astype(BF16), w_z=wi[:, s0:s1].astype(BF16),
        w_xbc=wi[:, s1:s2].astype(BF16), w_dt=wi[:, s2:s3].astype(BF16),
        w_gates=wi[:, s3:].astype(BF16),
        sgu_ln_g=sgu_ln_g[i], sgu_ln_b=sgu_ln_b[i], sgu_w=sgu_w[i], sgu_b=sgu_b[i],
        w_a=w_a[i].astype(BF16), conv_w=conv_w[i], conv_b=conv_b[i], dt_bias=dt_bias[i],
        a_log=a_log[i], d_skip=d_skip[i], ssm_norm_g=ssm_norm_g[i],
        w_b=w_b[i].astype(BF16), w_o=w_o[i].astype(BF16), norm_ffn_g=norm_ffn_g[i],
        w_gate=w_gate[i].astype(BF16), w_up=w_up[i].astype(BF16), w_down=w_down[i].astype(BF16),
    )


def kernel(x_prompt, x_sample, state_ssm, state_conv, norm_mix_g, w_in, sgu_ln_g, sgu_ln_b, sgu_w, sgu_b, w_a, conv_w, conv_b, dt_bias, a_log, d_skip, ssm_norm_g, w_b, w_o, norm_ffn_g, w_gate, w_up, w_down, norm_final_g):
    depth = w_in.shape[0]
    assert depth == 1, "multi-layer stacks need the final norm split from the FFN kernel"
    n_prompt = x_prompt.shape[0]
    n_heads = dt_bias.shape[1]
    conv_dim = conv_w.shape[2]
    p = _layer_params(0, norm_mix_g, w_in, sgu_ln_g, sgu_ln_b, sgu_w, sgu_b, w_a, conv_w, conv_b,
                      dt_bias, a_log, d_skip, ssm_norm_g, w_b, w_o, norm_ffn_g, w_gate, w_up,
                      w_down)
    h0_p = jnp.zeros((n_prompt, n_heads, SSM_HEADDIM, SSM_STATE), F32)
    c0_p = jnp.zeros((n_prompt, CONV_WIDTH - 1, conv_dim), F32)
    y_p, h_p, c_p, _ = _layer(x_prompt, h0_p, c0_p, p, norm_final_g)
    y_s, h_s, c_s, v_s = _layer(x_sample, state_ssm[0], state_conv[0], p, norm_final_g)
    return (y_p, y_s, h_p[None], c_p[None], h_s[None], c_s[None], v_s[None])
```

```python
import functools

import jax
import jax.numpy as jnp
from jax import lax
from jax.experimental import pallas as pl
from jax.experimental.pallas import tpu as pltpu

F32 = jnp.float32
BF16 = jnp.bfloat16

EPS = 1e-6
SGU_CHUNK = 128
SGU_GROUPS = 8
SSM_HEADDIM = 64
SSM_GROUPS = 8
SSM_STATE = 128
CONV_WIDTH = 4
SSD_CHUNK = 64

V7X_VMEM_LIMIT_BYTES = 56 * 1024 * 1024
LANES = 128
CONV_PAD_ROWS = 8
PROJ_EPILOGUE_ROWS = 32


def _cparams(*sem):
    return pltpu.CompilerParams(dimension_semantics=sem, vmem_limit_bytes=V7X_VMEM_LIMIT_BYTES)


def _dot(a, b):
    return jnp.dot(a, b, preferred_element_type=F32)


def _dot_nt(a, b):
    return lax.dot_general(a, b, (((1,), (1,)), ((), ())), preferred_element_type=F32)


def _rms(x, g):
    return x * lax.rsqrt(jnp.mean(x * x, axis=-1, keepdims=True) + EPS) * g


def _split_bf16(x):
    hi = x.astype(BF16)
    lo = (x - hi.astype(F32)).astype(BF16)
    return hi, lo


def _sigmoid(x):
    return 0.5 * jnp.tanh(0.5 * x) + 0.5


def _act_gelu(x):
    return jax.nn.gelu(x, approximate=True)


def _act_silu(x):
    return x * _sigmoid(x)


def _act_softplus(x):
    return jnp.maximum(x, 0.0) + jnp.log1p(jnp.exp(-jnp.abs(x)))


def _act_none(x):
    return x


def _rmsnorm_kernel(x_ref, g_ref, o_ref):
    o_ref[...] = _rms(x_ref[...], g_ref[...]).astype(o_ref.dtype)


def _rmsnorm_cast(x, g, tm):
    rows, d = x.shape
    return pl.pallas_call(
        _rmsnorm_kernel,
        grid=(rows // tm,),
        in_specs=[pl.BlockSpec((tm, d), lambda i: (i, 0)), pl.BlockSpec((1, d), lambda i: (0, 0))],
        out_specs=pl.BlockSpec((tm, d), lambda i: (i, 0)),
        out_shape=jax.ShapeDtypeStruct((rows, d), BF16),
        compiler_params=_cparams("arbitrary"),
        name="rmsnorm",
    )(x, g)


def _by_parity(n, body, acc0_ref, acc1_ref):
    @pl.when(n % 2 == 0)
    def _():
        body(acc1_ref, acc0_ref)

    @pl.when(n % 2 == 1)
    def _():
        body(acc0_ref, acc1_ref)


def _proj_kernel(*refs, act, has_bias):
    if has_bias:
        xn_ref, w_ref, b_ref, o_ref, acc0_ref, acc1_ref = refs
    else:
        xn_ref, w_ref, o_ref, acc0_ref, acc1_ref = refs
    n = pl.program_id(0)

    @pl.when(n == 0)
    def _():
        acc1_ref[...] = jnp.zeros_like(acc1_ref)

    tm = o_ref.shape[0]
    ec = min(tm, PROJ_EPILOGUE_ROWS)

    def body(prev_ref, next_ref):
        outs = []
        for r in range(0, tm, ec):
            prev = prev_ref[r:r + ec, :]
            if has_bias:
                prev = prev + b_ref[...]
            outs.append(act(prev).astype(o_ref.dtype))
        res = _dot(xn_ref[...], w_ref[...])
        for i, r in enumerate(range(0, tm, ec)):
            o_ref[r:r + ec, :] = outs[i]
        next_ref[...] = res

    _by_parity(n, body, acc0_ref, acc1_ref)


def _tile_maps(n_row, n_col):
    last = n_row * n_col - 1
    cur = lambda n: jnp.minimum(n, last)
    prv = lambda n: jnp.maximum(n - 1, 0)
    return cur, prv


def _proj(xn, w, bias, act, out_dtype, tm, tn):
    rows, d = xn.shape
    n_out = w.shape[1]
    n_row, n_col = rows // tm, n_out // tn
    cur, prv = _tile_maps(n_row, n_col)
    in_specs = [
        pl.BlockSpec((tm, d), lambda n: (cur(n) // n_col, 0)),
        pl.BlockSpec((d, tn), lambda n: (0, cur(n) % n_col)),
    ]
    args = [xn, w]
    if bias is not None:
        in_specs.append(pl.BlockSpec((1, tn), lambda n: (0, prv(n) % n_col)))
        args.append(bias)
    return pl.pallas_call(
        functools.partial(_proj_kernel, act=act, has_bias=bias is not None),
        grid=(n_row * n_col + 1,),
        in_specs=in_specs,
        out_specs=pl.BlockSpec((tm, tn), lambda n: (prv(n) // n_col, prv(n) % n_col)),
        out_shape=jax.ShapeDtypeStruct((rows, n_out), out_dtype),
        scratch_shapes=[pltpu.VMEM((tm, tn), F32), pltpu.VMEM((tm, tn), F32)],
        compiler_params=_cparams("arbitrary"),
        name="proj",
    )(*args)


def _proj_conv_kernel(xn_ref, w_ref, cw_ref, cb_ref, c0_ref, o_ref, tail_ref, acc0_ref, acc1_ref,
                      hist_ref, *, n_col, tiles_per_stream):
    n = pl.program_id(0)
    tm = o_ref.shape[0]
    pad = CONV_PAD_ROWS

    @pl.when(n == 0)
    def _():
        acc1_ref[...] = jnp.zeros_like(acc1_ref)
        hist_ref[...] = jnp.zeros_like(hist_ref)

    prv = jnp.maximum(n - 1, 0)
    jp = prv % n_col
    stream_start = ((prv // n_col) % tiles_per_stream) == 0

    ec = min(tm, PROJ_EPILOGUE_ROWS)

    def body(prev_ref, next_ref):
        prev_ref[0:pad, :] = jnp.where(stream_start, c0_ref[0], hist_ref[jp])
        last = prev_ref[tm:tm + pad, :]
        hist_ref[jp] = last
        tail_ref[0] = last
        outs = []
        for r in range(0, tm, ec):
            raw = prev_ref[r:r + pad + ec, :]
            conv = raw * cw_ref[CONV_WIDTH - 1:CONV_WIDTH, :] + cb_ref[...]
            for back in range(1, CONV_WIDTH):
                k = CONV_WIDTH - 1 - back
                conv = conv + pltpu.roll(raw, back, 0) * cw_ref[k:k + 1, :]
            outs.append(_act_silu(conv[pad:pad + ec, :]).astype(o_ref.dtype))
        res = _dot(xn_ref[...], w_ref[...])
        for i, r in enumerate(range(0, tm, ec)):
            o_ref[r:r + ec, :] = outs[i]
        next_ref[pad:pad + tm, :] = res

    _by_parity(n, body, acc0_ref, acc1_ref)


def _proj_conv(xn, w, conv_w, conv_b, c0_padded, n_streams, tm, tn):
    rows, d = xn.shape
    n_out = w.shape[1]
    n_row, n_col = rows // tm, n_out // tn
    tiles_per_stream = n_row // n_streams
    cur, prv = _tile_maps(n_row, n_col)
    stream = lambda n: prv(n) // n_col // tiles_per_stream
    return pl.pallas_call(
        functools.partial(_proj_conv_kernel, n_col=n_col, tiles_per_stream=tiles_per_stream),
        grid=(n_row * n_col + 1,),
        in_specs=[
            pl.BlockSpec((tm, d), lambda n: (cur(n) // n_col, 0)),
            pl.BlockSpec((d, tn), lambda n: (0, cur(n) % n_col)),
            pl.BlockSpec((CONV_WIDTH, tn), lambda n: (0, prv(n) % n_col)),
            pl.BlockSpec((1, tn), lambda n: (0, prv(n) % n_col)),
            pl.BlockSpec((1, CONV_PAD_ROWS, tn), lambda n: (stream(n), 0, prv(n) % n_col)),
        ],
        out_specs=[
            pl.BlockSpec((tm, tn), lambda n: (prv(n) // n_col, prv(n) % n_col)),
            pl.BlockSpec((1, CONV_PAD_ROWS, tn), lambda n: (prv(n) // n_col, 0, prv(n) % n_col)),
        ],
        out_shape=[
            jax.ShapeDtypeStruct((rows, n_out), BF16),
            jax.ShapeDtypeStruct((n_row, CONV_PAD_ROWS, n_out), F32),
        ],
        scratch_shapes=[
            pltpu.VMEM((CONV_PAD_ROWS + tm, tn), F32),
            pltpu.VMEM((CONV_PAD_ROWS + tm, tn), F32),
            pltpu.VMEM((n_col, CONV_PAD_ROWS, tn), F32),
        ],
        compiler_params=_cparams("arbitrary"),
        name="proj_conv",
    )(xn, w, conv_w, conv_b, c0_padded)


def _sgu_kernel(uv_ref, ga_ref, lng_ref, lnb_ref, wm_ref, bias_ref, wa_ref, o_ref, v_ref,
                gated_ref, *, seg_len):
    tm = uv_ref.shape[0]
    d = o_ref.shape[1]
    gd = d // SGU_GROUPS
    v = uv_ref[:, d:].astype(F32)
    mu = jnp.mean(v, axis=-1, keepdims=True)
    vc = v - mu
    var = jnp.mean(vc * vc, axis=-1, keepdims=True)
    vn = vc * lax.rsqrt(var + EPS) * lng_ref[...] + lnb_ref[...]
    v_ref[...] = vn
    vb = vn.astype(BF16)
    row = lax.broadcasted_iota(jnp.int32, (SGU_CHUNK, SGU_CHUNK), 0)
    col = lax.broadcasted_iota(jnp.int32, (SGU_CHUNK, SGU_CHUNK), 1)
    mask = (col <= row) & ((row // seg_len) == (col // seg_len))
    for g in range(SGU_GROUPS):
        w = jnp.where(mask, wm_ref[g], 0.0).astype(BF16)
        cs = slice(g * gd, (g + 1) * gd)
        for c in range(tm // SGU_CHUNK):
            rs = slice(c * SGU_CHUNK, (c + 1) * SGU_CHUNK)
            mixed = _dot(w, vb[rs, cs]) + bias_ref[:, cs]
            gated_ref[rs, cs] = (uv_ref[rs, cs].astype(F32) * mixed).astype(BF16)
    ya = _dot(gated_ref[...], wa_ref[...])
    o_ref[...] = (ga_ref[...].astype(F32) * ya).astype(o_ref.dtype)


def _sgu(uv, gates, ln_g, ln_b, wm, bias_full, w_a, seg_len, tm):
    rows = uv.shape[0]
    d = w_a.shape[0]
    return pl.pallas_call(
        functools.partial(_sgu_kernel, seg_len=seg_len),
        grid=(rows // tm,),
        in_specs=[
            pl.BlockSpec((tm, 2 * d), lambda i: (i, 0)),
            pl.BlockSpec((tm, d), lambda i: (i, 0)),
            pl.BlockSpec((1, d), lambda i: (0, 0)),
            pl.BlockSpec((1, d), lambda i: (0, 0)),
            pl.BlockSpec((SGU_GROUPS, SGU_CHUNK, SGU_CHUNK), lambda i: (0, 0, 0)),
            pl.BlockSpec((SGU_CHUNK, d), lambda i: (0, 0)),
            pl.BlockSpec((d, d), lambda i: (0, 0)),
        ],
        out_specs=[
            pl.BlockSpec((tm, d), lambda i: (i, 0)),
            pl.BlockSpec((tm, d), lambda i: (i, 0)),
        ],
        out_shape=[
            jax.ShapeDtypeStruct((rows, d), BF16),
            jax.ShapeDtypeStruct((rows, d), F32),
        ],
        scratch_shapes=[pltpu.VMEM((tm, d), BF16)],
        compiler_params=_cparams("arbitrary"),
        name="sgu",
    )(uv, gates, ln_g, ln_b, wm, bias_full, w_a)


def _ssd_kernel(*refs, n_tiles, valid_last, conv_in_kernel):
    if conv_in_kernel:
        (xbc_ref, dt_ref, z_ref, a_ref, d_ref, ng_ref, h0_ref, cw_ref, cb_ref, c0_ref,
         y_ref, hout_ref, cout_ref, ht_ref, cbuf_ref) = refs
    else:
        xbc_ref, dt_ref, z_ref, a_ref, d_ref, ng_ref, h0_ref, y_ref, hout_ref, ht_ref = refs
    t = pl.program_id(1)
    q = SSD_CHUNK
    d_ssm = y_ref.shape[1]
    n_heads = d_ssm // SSM_HEADDIM
    gw = d_ssm // SSM_GROUPS
    pair = 2 * SSM_HEADDIM
    n_bc = SSM_GROUPS * SSM_STATE
    assert pair == LANES and q == SSM_HEADDIM

    @pl.when(t == 0)
    def _init():
        for g in range(SSM_GROUPS):
            ht_ref[g] = h0_ref[0, g].T
        if conv_in_kernel:
            cbuf_ref[0:CONV_PAD_ROWS, :] = jnp.zeros((CONV_PAD_ROWS, cbuf_ref.shape[1]), F32)
            cbuf_ref[CONV_PAD_ROWS - (CONV_WIDTH - 1):CONV_PAD_ROWS, :] = c0_ref[0]

    if conv_in_kernel:
        cbuf_ref[CONV_PAD_ROWS:CONV_PAD_ROWS + q, :] = xbc_ref[...].astype(F32)
        conv = cb_ref[...]
        for k in range(CONV_WIDTH):
            r0 = CONV_PAD_ROWS - (CONV_WIDTH - 1) + k
            conv = conv + cbuf_ref[r0:r0 + q, :] * cw_ref[k:k + 1, :]

        @pl.when(t == n_tiles - 1)
        def _conv_out():
            r0 = CONV_PAD_ROWS + valid_last - (CONV_WIDTH - 1)
            cout_ref[0] = cbuf_ref[r0:r0 + CONV_WIDTH - 1, :]

        cbuf_ref[0:CONV_PAD_ROWS, :] = cbuf_ref[q:q + CONV_PAD_ROWS, :]
        xc = _act_silu(conv)
        xs = xc[:, :d_ssm]
        xs_bf = xs.astype(BF16)
        bm = xc[:, d_ssm:d_ssm + n_bc].astype(BF16)
        cm = xc[:, d_ssm + n_bc:].astype(BF16)
    else:
        xs_bf = xbc_ref[:, :d_ssm]
        xs = xs_bf.astype(F32)
        bm = xbc_ref[:, d_ssm:d_ssm + n_bc]
        cm = xbc_ref[:, d_ssm + n_bc:]

    dt = dt_ref[...]
    da = dt * a_ref[...]
    lane = lax.broadcasted_iota(jnp.int32, (q, d_ssm), 1)
    row = lax.broadcasted_iota(jnp.int32, (q, d_ssm), 0)
    pos = lane & (SSM_HEADDIM - 1)
    hrow = lax.broadcasted_iota(jnp.int32, (n_heads, d_ssm), 0)
    hlane = lax.broadcasted_iota(jnp.int32, (n_heads, d_ssm), 1)
    expand = (hrow == (hlane // SSM_HEADDIM)).astype(BF16)
    r2 = lax.broadcasted_iota(jnp.int32, (q, 2 * q), 0)
    c2 = lax.broadcasted_iota(jnp.int32, (q, 2 * q), 1) & (q - 1)
    lower2 = (c2 <= r2).astype(BF16)
    sub = CONV_PAD_ROWS
    ones1 = jnp.ones((sub, q), BF16)
    ones2 = jnp.ones((sub, 2 * q), BF16)

    da_hi, da_lo = _split_bf16(da)
    dab_hi = _dot(da_hi, expand)
    dab_lo = _dot(da_lo, expand)
    colb = _dot(lower2, jnp.concatenate([dab_hi.astype(BF16), dab_lo.astype(BF16)], axis=0))
    upto = row <= pos
    rowb = _dot(ones2, jnp.concatenate([jnp.where(upto, dab_hi, 0.0).astype(BF16),
                                        jnp.where(upto, dab_lo, 0.0).astype(BF16)], axis=0))[0:1]
    dtb = _dot(dt.astype(BF16), expand)
    dtrow = _dot(ones1, jnp.where(row == pos, dtb, 0.0).astype(BF16))[0:1]
    seg = colb - rowb
    mw = jnp.where(row >= pos, jnp.exp(jnp.minimum(seg, 0.0)), 0.0) * dtrow
    expc = jnp.exp(colb)
    decl = expc[q - 1:q, :]
    tailw = jnp.exp(jnp.minimum(colb[q - 1:q, :] - rowb, 0.0)) * dtrow

    eye = (lax.broadcasted_iota(jnp.int32, (SSM_STATE, SSM_STATE), 0)
           == lax.broadcasted_iota(jnp.int32, (SSM_STATE, SSM_STATE), 1)).astype(BF16)
    first_head = lax.broadcasted_iota(jnp.int32, (q, pair), 1) < SSM_HEADDIM

    for g in range(SSM_GROUPS):
        ns = slice(g * SSM_STATE, (g + 1) * SSM_STATE)
        bg = bm[:, ns]
        cg = cm[:, ns]
        bb = jnp.concatenate([bg, bg], axis=0)
        cb2 = _dot_nt(cg, bb)
        bt2 = _dot_nt(eye, bb)
        htg = ht_ref[g]
        yst = _dot(cg, htg.astype(BF16))
        ys = []
        for j in range(gw // pair):
            cs = slice(g * gw + j * pair, g * gw + (j + 1) * pair)
            js = slice(j * pair, (j + 1) * pair)
            m2 = (cb2 * mw[:, cs]).astype(BF16)
            xs2 = xs_bf[:, cs]
            zero = jnp.zeros_like(xs2)
            xbd = jnp.concatenate([jnp.where(first_head, xs2, zero),
                                   jnp.where(first_head, zero, xs2)], axis=0)
            y = _dot(m2, xbd) + expc[:, cs] * yst[:, js] + d_ref[:, cs] * xs[:, cs]
            btw = (bt2 * tailw[:, cs]).astype(BF16)
            ht_ref[g, :, js] = htg[:, js] * decl[:, cs] + _dot(btw, xbd)
            ys.append(y)
        gs = slice(g * gw, (g + 1) * gw)
        yg = jnp.concatenate(ys, axis=1) * z_ref[:, gs].astype(F32)
        y_ref[:, gs] = _rms(yg, ng_ref[:, gs]).astype(y_ref.dtype)

    @pl.when(t == n_tiles - 1)
    def _state_out():
        for g in range(SSM_GROUPS):
            hout_ref[0, g] = ht_ref[g].T


def _ssd(xbc, dt, zs, a_row, d_full, norm_g, h0, n_streams, n_tiles, valid_last, conv=None):
    rows, conv_dim = xbc.shape
    d_ssm = zs.shape[1]
    n_heads = dt.shape[1]
    gw = d_ssm // SSM_GROUPS
    q = SSD_CHUNK
    rb = lambda s, t: (s * n_tiles + t, 0)
    const2 = lambda s, t: (0, 0)
    per_stream3 = lambda s, t: (s, 0, 0)
    state_spec = pl.BlockSpec((1, SSM_GROUPS, gw, SSM_STATE), lambda s, t: (s, 0, 0, 0))
    in_specs = [
        pl.BlockSpec((q, conv_dim), rb),
        pl.BlockSpec((q, n_heads), rb),
        pl.BlockSpec((q, d_ssm), rb),
        pl.BlockSpec((1, n_heads), const2),
        pl.BlockSpec((1, d_ssm), const2),
        pl.BlockSpec((1, d_ssm), const2),
        state_spec,
    ]
    args = [xbc, dt, zs, a_row, d_full, norm_g, h0]
    out_specs = [pl.BlockSpec((q, d_ssm), rb), state_spec]
    out_shape = [
        jax.ShapeDtypeStruct((rows, d_ssm), BF16),
        jax.ShapeDtypeStruct((n_streams, SSM_GROUPS, gw, SSM_STATE), F32),
    ]
    scratch = [pltpu.VMEM((SSM_GROUPS, SSM_STATE, gw), F32)]
    if conv is not None:
        in_specs += [
            pl.BlockSpec((CONV_WIDTH, conv_dim), const2),
            pl.BlockSpec((1, conv_dim), const2),
            pl.BlockSpec((1, CONV_WIDTH - 1, conv_dim), per_stream3),
        ]
        args += list(conv)
        out_specs.append(pl.BlockSpec((1, CONV_WIDTH - 1, conv_dim), per_stream3))
        out_shape.append(jax.ShapeDtypeStruct((n_streams, CONV_WIDTH - 1, conv_dim), F32))
        scratch.append(pltpu.VMEM((CONV_PAD_ROWS + q, conv_dim), F32))
    return pl.pallas_call(
        functools.partial(_ssd_kernel, n_tiles=n_tiles, valid_last=valid_last,
                          conv_in_kernel=conv is not None),
        grid=(n_streams, n_tiles),
        in_specs=in_specs,
        out_specs=out_specs,
        out_shape=out_shape,
        scratch_shapes=scratch,
        compiler_params=_cparams("arbitrary", "arbitrary"),
        name="ssd",
    )(*args)


def _outproj_kernel(yn_ref, gaya_ref, gb_ref, x_ref, wb_ref, wo_ref, o_ref):
    yb = _dot(yn_ref[...], wb_ref[...])
    merged = gaya_ref[...].astype(F32) + gb_ref[...].astype(F32) * yb
    o_ref[...] = x_ref[...] + _dot(merged.astype(BF16), wo_ref[...])


def _outproj(yn, gaya, gates, x, w_b, w_o, tm):
    rows, d = x.shape
    d_ssm = yn.shape[1]
    return pl.pallas_call(
        _outproj_kernel,
        grid=(rows // tm,),
        in_specs=[
            pl.BlockSpec((tm, d_ssm), lambda i: (i, 0)),
            pl.BlockSpec((tm, d), lambda i: (i, 0)),
            pl.BlockSpec((tm, d), lambda i: (i, 1)),
            pl.BlockSpec((tm, d), lambda i: (i, 0)),
            pl.BlockSpec((d_ssm, d), lambda i: (0, 0), pipeline_mode=pl.Buffered(1)),
            pl.BlockSpec((d, d), lambda i: (0, 0), pipeline_mode=pl.Buffered(1)),
        ],
        out_specs=pl.BlockSpec((tm, d), lambda i: (i, 0)),
        out_shape=jax.ShapeDtypeStruct((rows, d), F32),
        compiler_params=_cparams("arbitrary"),
        name="outproj",
    )(yn, gaya, gates, x, w_b, w_o)


def _ffn_kernel(x_ref, g_ref, wg_ref, wu_ref, wd_ref, gf_ref, o_ref, hf_ref, acc_ref, *, n_ff):
    j = pl.program_id(1)

    @pl.when(j == 0)
    def _():
        hf_ref[...] = _rms(x_ref[...], g_ref[...]).astype(BF16)
        acc_ref[...] = jnp.zeros_like(acc_ref)

    hf = hf_ref[...]
    gate = _dot(hf, wg_ref[...])
    up = _dot(hf, wu_ref[...])
    mid = (gate * jax.nn.sigmoid(gate) * up).astype(BF16)
    acc_ref[...] += _dot(mid, wd_ref[...])

    @pl.when(j == n_ff - 1)
    def _():
        o_ref[...] = _rms(x_ref[...] + acc_ref[...], gf_ref[...])


def _ffn(x, g, w_gate, w_up, w_down, g_final, tm, tf):
    rows, d = x.shape
    d_ff = w_gate.shape[1]
    n_ff = d_ff // tf
    return pl.pallas_call(
        functools.partial(_ffn_kernel, n_ff=n_ff),
        grid=(rows // tm, n_ff),
        in_specs=[
            pl.BlockSpec((tm, d), lambda i, j: (i, 0)),
            pl.BlockSpec((1, d), lambda i, j: (0, 0)),
            pl.BlockSpec((d, tf), lambda i, j: (0, j)),
            pl.BlockSpec((d, tf), lambda i, j: (0, j)),
            pl.BlockSpec((tf, d), lambda i, j: (j, 0)),
            pl.BlockSpec((1, d), lambda i, j: (0, 0)),
        ],
        out_specs=pl.BlockSpec((tm, d), lambda i, j: (i, 0)),
        out_shape=jax.ShapeDtypeStruct((rows, d), F32),
        scratch_shapes=[pltpu.VMEM((tm, d), BF16), pltpu.VMEM((tm, d), F32)],
        compiler_params=_cparams("arbitrary", "arbitrary"),
        name="ffn",
    )(x, g, w_gate, w_up, w_down, g_final)


def _pick_tile(rows, preferred):
    tile = min(rows, preferred)
    assert rows % tile == 0, (rows, tile)
    return tile


def _layer(x, h0, c0, p, g_final):
    n_streams, length, d = x.shape
    rows = n_streams * length
    x2 = x.reshape(rows, d)
    d_ssm = p["w_b"].shape[0]
    n_heads = p["dt_bias"].shape[0]
    conv_dim = p["conv_w"].shape[1]
    tm = _pick_tile(rows, 1024)
    tn = 1024
    xn = _rmsnorm_cast(x2, p["norm_mix_g"].reshape(1, d), tm)
    uv = _proj(xn, p["w_uv"], None, _act_gelu, BF16, tm, tn)
    zs = _proj(xn, p["w_z"], None, _act_silu, BF16, tm, tn)
    dt = _proj(xn, p["w_dt"], p["dt_bias"].reshape(1, n_heads), _act_softplus, F32, tm, n_heads)
    gates = _proj(xn, p["w_gates"], None, _sigmoid, BF16, tm, tn)
    conv_b = p["conv_b"].reshape(1, conv_dim)
    fuse_conv = length % tm == 0
    if fuse_conv:
        c0_padded = jnp.pad(c0, ((0, 0), (CONV_PAD_ROWS - (CONV_WIDTH - 1), 0), (0, 0)))
        xbc, conv_tail = _proj_conv(xn, p["w_xbc"], p["conv_w"], conv_b, c0_padded, n_streams,
                                    tm, tn)
    else:
        xbc = _proj(xn, p["w_xbc"], None, _act_none, BF16, tm, tn)

    seg_len = min(length, SGU_CHUNK)
    assert SGU_CHUNK % seg_len == 0 and length % seg_len == 0
    rep = SGU_CHUNK // seg_len
    wm = jnp.tile(p["sgu_w"][:, :seg_len, :seg_len], (1, rep, rep))
    bias_full = jnp.repeat(jnp.tile(p["sgu_b"][:, :seg_len], (1, rep)).T, d // SGU_GROUPS, axis=1)
    gaya, v_norm = _sgu(uv, gates, p["sgu_ln_g"].reshape(1, d), p["sgu_ln_b"].reshape(1, d),
                        wm, bias_full, p["w_a"], seg_len, _pick_tile(rows, 512))

    q = SSD_CHUNK
    n_tiles = -(-length // q)
    padded = n_tiles * q
    valid_last = length - (n_tiles - 1) * q

    def pad_rows(arr):
        if padded == length:
            return arr
        arr = arr.reshape(n_streams, length, arr.shape[-1])
        arr = jnp.pad(arr, ((0, 0), (0, padded - length), (0, 0)))
        return arr.reshape(n_streams * padded, arr.shape[-1])

    gw = d_ssm // SSM_GROUPS
    a_row = (-jnp.exp(p["a_log"].astype(F32))).reshape(1, n_heads)
    d_full = jnp.repeat(p["d_skip"].astype(F32), SSM_HEADDIM).reshape(1, d_ssm)
    ssd = functools.partial(_ssd, pad_rows(xbc), pad_rows(dt), pad_rows(zs), a_row, d_full,
                            p["ssm_norm_g"].reshape(1, d_ssm),
                            h0.reshape(n_streams, SSM_GROUPS, gw, SSM_STATE),
                            n_streams, n_tiles, valid_last)
    if fuse_conv:
        yn, h_new = ssd()
        tiles_per_stream = length // tm
        c_new = conv_tail[tiles_per_stream - 1::tiles_per_stream,
                          CONV_PAD_ROWS - (CONV_WIDTH - 1):, :]
    else:
        yn, h_new, c_new = ssd(conv=(p["conv_w"], conv_b, c0))
    if padded != length:
        yn = yn.reshape(n_streams, padded, d_ssm)[:, :length].reshape(rows, d_ssm)
    h_new = h_new.reshape(n_streams, n_heads, SSM_HEADDIM, SSM_STATE)

    x1 = _outproj(yn, gaya, gates, x2, p["w_b"], p["w_o"], _pick_tile(rows, 256))
    y = _ffn(x1, p["norm_ffn_g"].reshape(1, d), p["w_gate"], p["w_up"], p["w_down"],
             g_final.reshape(1, d), _pick_tile(rows, 512), 512)
    return (y.reshape(n_streams, length, d), h_new, c_new,
            v_norm.reshape(n_streams, length, d))


def _layer_params(i, norm_mix_g, w_in, sgu_ln_g, sgu_ln_b, sgu_w, sgu_b, w_a, conv_w, conv_b,
                  dt_bias, a_log, d_skip, ssm_norm_g, w_b, w_o, norm_ffn_g, w_gate, w_up, w_down):
    d = w_in.shape[1]
    d_ssm = w_b.shape[1]
    conv_dim = conv_w.shape[2]
    n_heads = dt_bias.shape[1]
    s0 = 2 * d
    s1 = s0 + d_ssm
    s2 = s1 + conv_dim
    s3 = s2 + n_heads
    wi = w_in[i]
    return dict(
        norm_mix_g=norm_mix_g[i],
        w_uv=wi[:, :s0].astype(BF16), w_z=wi[:, s0:s1].astype(BF16),
        w_xbc=wi[:, s1:s2].astype(BF16), w_dt=wi[:, s2:s3].astype(BF16),
        w_gates=wi[:, s3:].astype(BF16),
        sgu_ln_g=sgu_ln_g[i], sgu_ln_b=sgu_ln_b[i], sgu_w=sgu_w[i], sgu_b=sgu_b[i],
        w_a=w_a[i].astype(BF16), conv_w=conv_w[i], conv_b=conv_b[i], dt_bias=dt_bias[i],
        a_log=a_log[i], d_skip=d_skip[i], ssm_norm_g=ssm_norm_g[i],
        w_b=w_b[i].astype(BF16), w_o=w_o[i].astype(BF16), norm_ffn_g=norm_ffn_g[i],
        w_gate=w_gate[i].astype(BF16), w_up=w_up[i].astype(BF16), w_down=w_down[i].astype(BF16),
    )


def kernel(x_prompt, x_sample, state_ssm, state_conv, norm_mix_g, w_in, sgu_ln_g, sgu_ln_b, sgu_w, sgu_b, w_a, conv_w, conv_b, dt_bias, a_log, d_skip, ssm_norm_g, w_b, w_o, norm_ffn_g, w_gate, w_up, w_down, norm_final_g):
    depth = w_in.shape[0]
    assert depth == 1, "multi-layer stacks need the final norm split from the FFN kernel"
    n_prompt = x_prompt.shape[0]
    n_heads = dt_bias.shape[1]
    conv_dim = conv_w.shape[2]
    p = _layer_params(0, norm_mix_g, w_in, sgu_ln_g, sgu_ln_b, sgu_w, sgu_b, w_a, conv_w, conv_b,
                      dt_bias, a_log, d_skip, ssm_norm_g, w_b, w_o, norm_ffn_g, w_gate, w_up,
                      w_down)
    h0_p = jnp.zeros((n_prompt, n_heads, SSM_HEADDIM, SSM_STATE), F32)
    c0_p = jnp.zeros((n_prompt, CONV_WIDTH - 1, conv_dim), F32)
    y_p, h_p, c_p, _ = _layer(x_prompt, h0_p, c0_p, p, norm_final_g)
    y_s, h_s, c_s, v_s = _layer(x_sample, state_ssm[0], state_conv[0], p, norm_final_g)
    return (y_p, y_s, h_p[None], c_p[None], h_s[None], c_s[None], v_s[None])
```

```python
import functools

import jax
import jax.numpy as jnp
from jax import lax
from jax.experimental import pallas as pl
from jax.experimental.pallas import tpu as pltpu

F32 = jnp.float32
BF16 = jnp.bfloat16

EPS = 1e-6
SGU_CHUNK = 128
SGU_GROUPS = 8
SSM_HEADDIM = 64
SSM_GROUPS = 8
SSM_STATE = 128
CONV_WIDTH = 4
SSD_CHUNK = 64
SSD_CHUNKS_PER_STEP = 2

V7X_VMEM_LIMIT_BYTES = 56 * 1024 * 1024
LANES = 128
CONV_PAD_ROWS = 8
PROJ_EPILOGUE_ROWS = 32


def _cparams(*sem):
    return pltpu.CompilerParams(dimension_semantics=sem, vmem_limit_bytes=V7X_VMEM_LIMIT_BYTES)


def _dot(a, b):
    return jnp.dot(a, b, preferred_element_type=F32)


def _dot_nt(a, b):
    return lax.dot_general(a, b, (((1,), (1,)), ((), ())), preferred_element_type=F32)


def _rms(x, g):
    return x * lax.rsqrt(jnp.mean(x * x, axis=-1, keepdims=True) + EPS) * g


def _split_bf16(x):
    hi = x.astype(BF16)
    lo = (x - hi.astype(F32)).astype(BF16)
    return hi, lo


def _sigmoid(x):
    return 0.5 * jnp.tanh(0.5 * x) + 0.5


def _act_gelu(x):
    return jax.nn.gelu(x, approximate=True)


def _act_silu(x):
    return x * _sigmoid(x)


def _act_softplus(x):
    return jnp.maximum(x, 0.0) + jnp.log1p(jnp.exp(-jnp.abs(x)))


def _act_none(x):
    return x


def _rmsnorm_kernel(x_ref, g_ref, o_ref):
    o_ref[...] = _rms(x_ref[...], g_ref[...]).astype(o_ref.dtype)


def _rmsnorm_cast(x, g, tm):
    rows, d = x.shape
    return pl.pallas_call(
        _rmsnorm_kernel,
        grid=(rows // tm,),
        in_specs=[pl.BlockSpec((tm, d), lambda i: (i, 0)), pl.BlockSpec((1, d), lambda i: (0, 0))],
        out_specs=pl.BlockSpec((tm, d), lambda i: (i, 0)),
        out_shape=jax.ShapeDtypeStruct((rows, d), BF16),
        compiler_params=_cparams("arbitrary"),
        name="rmsnorm",
    )(x, g)


def _by_parity(n, body, acc0_ref, acc1_ref):
    @pl.when(n % 2 == 0)
    def _():
        body(acc1_ref, acc0_ref)

    @pl.when(n % 2 == 1)
    def _():
        body(acc0_ref, acc1_ref)


def _proj_kernel(*refs, act, has_bias):
    if has_bias:
        xn_ref, w_ref, b_ref, o_ref, acc0_ref, acc1_ref = refs
    else:
        xn_ref, w_ref, o_ref, acc0_ref, acc1_ref = refs
    n = pl.program_id(0)

    @pl.when(n == 0)
    def _():
        acc1_ref[...] = jnp.zeros_like(acc1_ref)

    tm = o_ref.shape[0]
    ec = min(tm, PROJ_EPILOGUE_ROWS)

    def body(prev_ref, next_ref):
        outs = []
        for r in range(0, tm, ec):
            prev = prev_ref[r:r + ec, :]
            if has_bias:
                prev = prev + b_ref[...]
            outs.append(act(prev).astype(o_ref.dtype))
        res = _dot(xn_ref[...], w_ref[...])
        for i, r in enumerate(range(0, tm, ec)):
            o_ref[r:r + ec, :] = outs[i]
        next_ref[...] = res

    _by_parity(n, body, acc0_ref, acc1_ref)


def _tile_maps(n_row, n_col):
    last = n_row * n_col - 1
    cur = lambda n: jnp.minimum(n, last)
    prv = lambda n: jnp.maximum(n - 1, 0)
    return cur, prv


def _proj(xn, w, bias, act, out_dtype, tm, tn):
    rows, d = xn.shape
    n_out = w.shape[1]
    n_row, n_col = rows // tm, n_out // tn
    cur, prv = _tile_maps(n_row, n_col)
    in_specs = [
        pl.BlockSpec((tm, d), lambda n: (cur(n) // n_col, 0)),
        pl.BlockSpec((d, tn), lambda n: (0, cur(n) % n_col)),
    ]
    args = [xn, w]
    if bias is not None:
        in_specs.append(pl.BlockSpec((1, tn), lambda n: (0, prv(n) % n_col)))
        args.append(bias)
    return pl.pallas_call(
        functools.partial(_proj_kernel, act=act, has_bias=bias is not None),
        grid=(n_row * n_col + 1,),
        in_specs=in_specs,
        out_specs=pl.BlockSpec((tm, tn), lambda n: (prv(n) // n_col, prv(n) % n_col)),
        out_shape=jax.ShapeDtypeStruct((rows, n_out), out_dtype),
        scratch_shapes=[pltpu.VMEM((tm, tn), F32), pltpu.VMEM((tm, tn), F32)],
        compiler_params=_cparams("arbitrary"),
        name="proj",
    )(*args)


def _proj_conv_kernel(xn_ref, w_ref, cw_ref, cb_ref, c0_ref, o_ref, tail_ref, acc0_ref, acc1_ref,
                      hist_ref, *, n_col, tiles_per_stream):
    n = pl.program_id(0)
    tm = o_ref.shape[0]
    pad = CONV_PAD_ROWS

    @pl.when(n == 0)
    def _():
        acc1_ref[...] = jnp.zeros_like(acc1_ref)
        hist_ref[...] = jnp.zeros_like(hist_ref)

    prv = jnp.maximum(n - 1, 0)
    jp = prv % n_col
    stream_start = ((prv // n_col) % tiles_per_stream) == 0

    ec = min(tm, PROJ_EPILOGUE_ROWS)

    def body(prev_ref, next_ref):
        prev_ref[0:pad, :] = jnp.where(stream_start, c0_ref[0], hist_ref[jp])
        last = prev_ref[tm:tm + pad, :]
        hist_ref[jp] = last
        tail_ref[0] = last
        outs = []
        for r in range(0, tm, ec):
            raw = prev_ref[r:r + pad + ec, :]
            conv = raw * cw_ref[CONV_WIDTH - 1:CONV_WIDTH, :] + cb_ref[...]
            for back in range(1, CONV_WIDTH):
                k = CONV_WIDTH - 1 - back
                conv = conv + pltpu.roll(raw, back, 0) * cw_ref[k:k + 1, :]
            outs.append(_act_silu(conv[pad:pad + ec, :]).astype(o_ref.dtype))
        res = _dot(xn_ref[...], w_ref[...])
        for i, r in enumerate(range(0, tm, ec)):
            o_ref[r:r + ec, :] = outs[i]
        next_ref[pad:pad + tm, :] = res

    _by_parity(n, body, acc0_ref, acc1_ref)


def _proj_conv(xn, w, conv_w, conv_b, c0_padded, n_streams, tm, tn):
    rows, d = xn.shape
    n_out = w.shape[1]
    n_row, n_col = rows // tm, n_out // tn
    tiles_per_stream = n_row // n_streams
    cur, prv = _tile_maps(n_row, n_col)
    stream = lambda n: prv(n) // n_col // tiles_per_stream
    return pl.pallas_call(
        functools.partial(_proj_conv_kernel, n_col=n_col, tiles_per_stream=tiles_per_stream),
        grid=(n_row * n_col + 1,),
        in_specs=[
            pl.BlockSpec((tm, d), lambda n: (cur(n) // n_col, 0)),
            pl.BlockSpec((d, tn), lambda n: (0, cur(n) % n_col)),
            pl.BlockSpec((CONV_WIDTH, tn), lambda n: (0, prv(n) % n_col)),
            pl.BlockSpec((1, tn), lambda n: (0, prv(n) % n_col)),
            pl.BlockSpec((1, CONV_PAD_ROWS, tn), lambda n: (stream(n), 0, prv(n) % n_col)),
        ],
        out_specs=[
            pl.BlockSpec((tm, tn), lambda n: (prv(n) // n_col, prv(n) % n_col)),
            pl.BlockSpec((1, CONV_PAD_ROWS, tn), lambda n: (prv(n) // n_col, 0, prv(n) % n_col)),
        ],
        out_shape=[
            jax.ShapeDtypeStruct((rows, n_out), BF16),
            jax.ShapeDtypeStruct((n_row, CONV_PAD_ROWS, n_out), F32),
        ],
        scratch_shapes=[
            pltpu.VMEM((CONV_PAD_ROWS + tm, tn), F32),
            pltpu.VMEM((CONV_PAD_ROWS + tm, tn), F32),
            pltpu.VMEM((n_col, CONV_PAD_ROWS, tn), F32),
        ],
        compiler_params=_cparams("arbitrary"),
        name="proj_conv",
    )(xn, w, conv_w, conv_b, c0_padded)


def _sgu_kernel(uv_ref, ga_ref, lng_ref, lnb_ref, wm_ref, bias_ref, wa_ref, o_ref, v_ref,
                gated_ref, *, seg_len):
    tm = uv_ref.shape[0]
    d = o_ref.shape[1]
    gd = d // SGU_GROUPS
    v = uv_ref[:, d:].astype(F32)
    mu = jnp.mean(v, axis=-1, keepdims=True)
    vc = v - mu
    var = jnp.mean(vc * vc, axis=-1, keepdims=True)
    vn = vc * lax.rsqrt(var + EPS) * lng_ref[...] + lnb_ref[...]
    v_ref[...] = vn
    vb = vn.astype(BF16)
    row = lax.broadcasted_iota(jnp.int32, (SGU_CHUNK, SGU_CHUNK), 0)
    col = lax.broadcasted_iota(jnp.int32, (SGU_CHUNK, SGU_CHUNK), 1)
    mask = (col <= row) & ((row // seg_len) == (col // seg_len))
    for g in range(SGU_GROUPS):
        w = jnp.where(mask, wm_ref[g], 0.0).astype(BF16)
        cs = slice(g * gd, (g + 1) * gd)
        for c in range(tm // SGU_CHUNK):
            rs = slice(c * SGU_CHUNK, (c + 1) * SGU_CHUNK)
            mixed = _dot(w, vb[rs, cs]) + bias_ref[:, cs]
            gated_ref[rs, cs] = (uv_ref[rs, cs].astype(F32) * mixed).astype(BF16)
    ya = _dot(gated_ref[...], wa_ref[...])
    o_ref[...] = (ga_ref[...].astype(F32) * ya).astype(o_ref.dtype)


def _sgu(uv, gates, ln_g, ln_b, wm, bias_full, w_a, seg_len, tm):
    rows = uv.shape[0]
    d = w_a.shape[0]
    return pl.pallas_call(
        functools.partial(_sgu_kernel, seg_len=seg_len),
        grid=(rows // tm,),
        in_specs=[
            pl.BlockSpec((tm, 2 * d), lambda i: (i, 0)),
            pl.BlockSpec((tm, d), lambda i: (i, 0)),
            pl.BlockSpec((1, d), lambda i: (0, 0)),
            pl.BlockSpec((1, d), lambda i: (0, 0)),
            pl.BlockSpec((SGU_GROUPS, SGU_CHUNK, SGU_CHUNK), lambda i: (0, 0, 0)),
            pl.BlockSpec((SGU_CHUNK, d), lambda i: (0, 0)),
            pl.BlockSpec((d, d), lambda i: (0, 0)),
        ],
        out_specs=[
            pl.BlockSpec((tm, d), lambda i: (i, 0)),
            pl.BlockSpec((tm, d), lambda i: (i, 0)),
        ],
        out_shape=[
            jax.ShapeDtypeStruct((rows, d), BF16),
            jax.ShapeDtypeStruct((rows, d), F32),
        ],
        scratch_shapes=[pltpu.VMEM((tm, d), BF16)],
        compiler_params=_cparams("arbitrary"),
        name="sgu",
    )(uv, gates, ln_g, ln_b, wm, bias_full, w_a)


def _ssd_kernel(*refs, n_tiles, valid_last, conv_in_kernel, chunks):
    if conv_in_kernel:
        (xbc_ref, dt_ref, z_ref, a_ref, d_ref, ng_ref, h0_ref, cw_ref, cb_ref, c0_ref,
         y_ref, hout_ref, cout_ref, ht_ref, cbuf_ref) = refs
    else:
        xbc_ref, dt_ref, z_ref, a_ref, d_ref, ng_ref, h0_ref, y_ref, hout_ref, ht_ref = refs
    t = pl.program_id(1)
    q = SSD_CHUNK
    d_ssm = y_ref.shape[1]
    n_heads = d_ssm // SSM_HEADDIM
    gw = d_ssm // SSM_GROUPS
    pair = 2 * SSM_HEADDIM
    n_bc = SSM_GROUPS * SSM_STATE
    assert pair == LANES and q == SSM_HEADDIM

    @pl.when(t == 0)
    def _init():
        for g in range(SSM_GROUPS):
            ht_ref[g] = h0_ref[0, g].T
        if conv_in_kernel:
            cbuf_ref[0:CONV_PAD_ROWS, :] = jnp.zeros((CONV_PAD_ROWS, cbuf_ref.shape[1]), F32)
            cbuf_ref[CONV_PAD_ROWS - (CONV_WIDTH - 1):CONV_PAD_ROWS, :] = c0_ref[0]

    tq = chunks * q
    n_pairs = gw // pair
    if conv_in_kernel:
        cbuf_ref[CONV_PAD_ROWS:CONV_PAD_ROWS + tq, :] = xbc_ref[...].astype(F32)
        conv = cb_ref[...]
        for k in range(CONV_WIDTH):
            r0 = CONV_PAD_ROWS - (CONV_WIDTH - 1) + k
            conv = conv + cbuf_ref[r0:r0 + tq, :] * cw_ref[k:k + 1, :]

        @pl.when(t == n_tiles - 1)
        def _conv_out():
            r0 = CONV_PAD_ROWS + valid_last - (CONV_WIDTH - 1)
            cout_ref[0] = cbuf_ref[r0:r0 + CONV_WIDTH - 1, :]

        cbuf_ref[0:CONV_PAD_ROWS, :] = cbuf_ref[tq:tq + CONV_PAD_ROWS, :]
        xc_all = _act_silu(conv).astype(BF16)
    else:
        xc_all = xbc_ref[...]

    row = lax.broadcasted_iota(jnp.int32, (q, gw), 0)
    pos = lax.broadcasted_iota(jnp.int32, (q, gw), 1) & (SSM_HEADDIM - 1)
    on_diag = row == pos
    causal = row >= pos
    hrow = lax.broadcasted_iota(jnp.int32, (n_heads, d_ssm), 0)
    hlane = lax.broadcasted_iota(jnp.int32, (n_heads, d_ssm), 1)
    expand = (hrow == (hlane // SSM_HEADDIM)).astype(BF16)
    r2 = lax.broadcasted_iota(jnp.int32, (q, q), 0)
    c2 = lax.broadcasted_iota(jnp.int32, (q, q), 1)
    lower = (c2 <= r2).astype(BF16)
    eye = (lax.broadcasted_iota(jnp.int32, (SSM_STATE, SSM_STATE), 0)
           == lax.broadcasted_iota(jnp.int32, (SSM_STATE, SSM_STATE), 1)).astype(BF16)
    first_head = lax.broadcasted_iota(jnp.int32, (q, pair), 1) < SSM_HEADDIM

    state = []
    for g in range(SSM_GROUPS):
        htg = ht_ref[g]
        state.append([htg[:, j * pair:(j + 1) * pair] for j in range(n_pairs)])
    y_out = []

    for c in range(chunks):
        rs = slice(c * q, (c + 1) * q)
        dt = dt_ref[rs, :]
        dt_bf = dt.astype(BF16)
        da_hi, da_lo = _split_bf16(dt * a_ref[...])
        cum_hi, cum_lo = _split_bf16(_dot(lower, da_hi) + _dot(lower, da_lo))

        y_groups = []
        for g in range(SSM_GROUPS):
            gs = slice(g * gw, (g + 1) * gw)
            ns = slice(d_ssm + g * SSM_STATE, d_ssm + (g + 1) * SSM_STATE)
            colb = _dot(cum_hi, expand[:, gs]) + _dot(cum_lo, expand[:, gs])
            rowb = jnp.sum(jnp.where(on_diag, colb, 0.0), axis=0, keepdims=True)
            mw = jnp.where(causal, jnp.exp(jnp.minimum(colb - rowb, 0.0)), 0.0)
            expc = jnp.exp(colb)
            decl = expc[q - 1:q, :]
            tailw = jnp.exp(jnp.minimum(colb[q - 1:q, :] - rowb, 0.0))
            xs = xc_all[rs, gs].astype(F32)
            xdt = (xs * _dot(dt_bf, expand[:, gs])).astype(BF16)

            bg = xc_all[rs, ns]
            cg = xc_all[rs, n_bc + ns.start:n_bc + ns.stop]
            bb = jnp.concatenate([bg, bg], axis=0)
            cb2 = _dot_nt(cg, bb)
            bt2 = _dot_nt(eye, bb)
            yst = _dot(cg, jnp.concatenate(state[g], axis=1).astype(BF16))
            ys = []
            for j in range(n_pairs):
                js = slice(j * pair, (j + 1) * pair)
                m2 = (cb2 * mw[:, js]).astype(BF16)
                xd2 = xdt[:, js]
                zero = jnp.zeros_like(xd2)
                xbd = jnp.concatenate([jnp.where(first_head, xd2, zero),
                                       jnp.where(first_head, zero, xd2)], axis=0)
                ys.append(_dot(m2, xbd) + expc[:, js] * yst[:, js]
                          + d_ref[:, g * gw + j * pair:g * gw + (j + 1) * pair] * xs[:, js])
                btw = (bt2 * tailw[:, js]).astype(BF16)
                state[g][j] = state[g][j] * decl[:, js] + _dot(btw, xbd)
            yg = jnp.concatenate(ys, axis=1) * z_ref[rs, gs].astype(F32)
            y_groups.append(_rms(yg, ng_ref[:, gs]).astype(y_ref.dtype))
        y_out.append(y_groups)

    for c in range(chunks):
        for g in range(SSM_GROUPS):
            y_ref[c * q:(c + 1) * q, g * gw:(g + 1) * gw] = y_out[c][g]
    for g in range(SSM_GROUPS):
        ht_ref[g] = jnp.concatenate(state[g], axis=1)

    @pl.when(t == n_tiles - 1)
    def _state_out():
        for g in range(SSM_GROUPS):
            hout_ref[0, g] = ht_ref[g].T


def _ssd(xbc, dt, zs, a_row, d_full, norm_g, h0, n_streams, n_tiles, valid_last, chunks,
         conv=None):
    rows, conv_dim = xbc.shape
    d_ssm = zs.shape[1]
    n_heads = dt.shape[1]
    gw = d_ssm // SSM_GROUPS
    q = chunks * SSD_CHUNK
    rb = lambda s, t: (s * n_tiles + t, 0)
    const2 = lambda s, t: (0, 0)
    per_stream3 = lambda s, t: (s, 0, 0)
    state_spec = pl.BlockSpec((1, SSM_GROUPS, gw, SSM_STATE), lambda s, t: (s, 0, 0, 0))
    in_specs = [
        pl.BlockSpec((q, conv_dim), rb),
        pl.BlockSpec((q, n_heads), rb),
        pl.BlockSpec((q, d_ssm), rb),
        pl.BlockSpec((1, n_heads), const2),
        pl.BlockSpec((1, d_ssm), const2),
        pl.BlockSpec((1, d_ssm), const2),
        state_spec,
    ]
    args = [xbc, dt, zs, a_row, d_full, norm_g, h0]
    out_specs = [pl.BlockSpec((q, d_ssm), rb), state_spec]
    out_shape = [
        jax.ShapeDtypeStruct((rows, d_ssm), BF16),
        jax.ShapeDtypeStruct((n_streams, SSM_GROUPS, gw, SSM_STATE), F32),
    ]
    scratch = [pltpu.VMEM((SSM_GROUPS, SSM_STATE, gw), F32)]
    if conv is not None:
        in_specs += [
            pl.BlockSpec((CONV_WIDTH, conv_dim), const2),
            pl.BlockSpec((1, conv_dim), const2),
            pl.BlockSpec((1, CONV_WIDTH - 1, conv_dim), per_stream3),
        ]
        args += list(conv)
        out_specs.append(pl.BlockSpec((1, CONV_WIDTH - 1, conv_dim), per_stream3))
        out_shape.append(jax.ShapeDtypeStruct((n_streams, CONV_WIDTH - 1, conv_dim), F32))
        scratch.append(pltpu.VMEM((CONV_PAD_ROWS + q, conv_dim), F32))
    return pl.pallas_call(
        functools.partial(_ssd_kernel, n_tiles=n_tiles, valid_last=valid_last,
                          conv_in_kernel=conv is not None, chunks=chunks),
        grid=(n_streams, n_tiles),
        in_specs=in_specs,
        out_specs=out_specs,
        out_shape=out_shape,
        scratch_shapes=scratch,
        compiler_params=_cparams("arbitrary", "arbitrary"),
        name="ssd",
    )(*args)


def _outproj_kernel(yn_ref, gaya_ref, gb_ref, x_ref, wb_ref, wo_ref, o_ref):
    yb = _dot(yn_ref[...], wb_ref[...])
    merged = gaya_ref[...].astype(F32) + gb_ref[...].astype(F32) * yb
    o_ref[...] = x_ref[...] + _dot(merged.astype(BF16), wo_ref[...])


def _outproj(yn, gaya, gates, x, w_b, w_o, tm):
    rows, d = x.shape
    d_ssm = yn.shape[1]
    return pl.pallas_call(
        _outproj_kernel,
        grid=(rows // tm,),
        in_specs=[
            pl.BlockSpec((tm, d_ssm), lambda i: (i, 0)),
            pl.BlockSpec((tm, d), lambda i: (i, 0)),
            pl.BlockSpec((tm, d), lambda i: (i, 1)),
            pl.BlockSpec((tm, d), lambda i: (i, 0)),
            pl.BlockSpec((d_ssm, d), lambda i: (0, 0), pipeline_mode=pl.Buffered(1)),
            pl.BlockSpec((d, d), lambda i: (0, 0), pipeline_mode=pl.Buffered(1)),
        ],
        out_specs=pl.BlockSpec((tm, d), lambda i: (i, 0)),
        out_shape=jax.ShapeDtypeStruct((rows, d), F32),
        compiler_params=_cparams("arbitrary"),
        name="outproj",
    )(yn, gaya, gates, x, w_b, w_o)


def _ffn_kernel(x_ref, g_ref, wg_ref, wu_ref, wd_ref, gf_ref, o_ref, hf_ref, acc_ref, *, n_ff):
    j = pl.program_id(1)

    @pl.when(j == 0)
    def _():
        hf_ref[...] = _rms(x_ref[...], g_ref[...]).astype(BF16)
        acc_ref[...] = jnp.zeros_like(acc_ref)

    hf = hf_ref[...]
    gate = _dot(hf, wg_ref[...])
    up = _dot(hf, wu_ref[...])
    mid = (gate * jax.nn.sigmoid(gate) * up).astype(BF16)
    acc_ref[...] += _dot(mid, wd_ref[...])

    @pl.when(j == n_ff - 1)
    def _():
        o_ref[...] = _rms(x_ref[...] + acc_ref[...], gf_ref[...])


def _ffn(x, g, w_gate, w_up, w_down, g_final, tm, tf):
    rows, d = x.shape
    d_ff = w_gate.shape[1]
    n_ff = d_ff // tf
    return pl.pallas_call(
        functools.partial(_ffn_kernel, n_ff=n_ff),
        grid=(rows // tm, n_ff),
        in_specs=[
            pl.BlockSpec((tm, d), lambda i, j: (i, 0)),
            pl.BlockSpec((1, d), lambda i, j: (0, 0)),
            pl.BlockSpec((d, tf), lambda i, j: (0, j)),
            pl.BlockSpec((d, tf), lambda i, j: (0, j)),
            pl.BlockSpec((tf, d), lambda i, j: (j, 0)),
            pl.BlockSpec((1, d), lambda i, j: (0, 0)),
        ],
        out_specs=pl.BlockSpec((tm, d), lambda i, j: (i, 0)),
        out_shape=jax.ShapeDtypeStruct((rows, d), F32),
        scratch_shapes=[pltpu.VMEM((tm, d), BF16), pltpu.VMEM((tm, d), F32)],
        compiler_params=_cparams("arbitrary", "arbitrary"),
        name="ffn",
    )(x, g, w_gate, w_up, w_down, g_final)


def _pick_tile(rows, preferred):
    tile = min(rows, preferred)
    assert rows % tile == 0, (rows, tile)
    return tile


def _layer(x, h0, c0, p, g_final):
    n_streams, length, d = x.shape
    rows = n_streams * length
    x2 = x.reshape(rows, d)
    d_ssm = p["w_b"].shape[0]
    n_heads = p["dt_bias"].shape[0]
    conv_dim = p["conv_w"].shape[1]
    tm = _pick_tile(rows, 1024)
    tn = 1024
    xn = _rmsnorm_cast(x2, p["norm_mix_g"].reshape(1, d), tm)
    uv = _proj(xn, p["w_uv"], None, _act_gelu, BF16, tm, tn)
    zs = _proj(xn, p["w_z"], None, _act_silu, BF16, tm, tn)
    dt = _proj(xn, p["w_dt"], p["dt_bias"].reshape(1, n_heads), _act_softplus, F32, tm, n_heads)
    gates = _proj(xn, p["w_gates"], None, _sigmoid, BF16, tm, tn)
    conv_b = p["conv_b"].reshape(1, conv_dim)
    fuse_conv = length % tm == 0
    if fuse_conv:
        c0_padded = jnp.pad(c0, ((0, 0), (CONV_PAD_ROWS - (CONV_WIDTH - 1), 0), (0, 0)))
        xbc, conv_tail = _proj_conv(xn, p["w_xbc"], p["conv_w"], conv_b, c0_padded, n_streams,
                                    tm, tn)
    else:
        xbc = _proj(xn, p["w_xbc"], None, _act_none, BF16, tm, tn)

    seg_len = min(length, SGU_CHUNK)
    assert SGU_CHUNK % seg_len == 0 and length % seg_len == 0
    rep = SGU_CHUNK // seg_len
    wm = jnp.tile(p["sgu_w"][:, :seg_len, :seg_len], (1, rep, rep))
    bias_full = jnp.repeat(jnp.tile(p["sgu_b"][:, :seg_len], (1, rep)).T, d // SGU_GROUPS, axis=1)
    gaya, v_norm = _sgu(uv, gates, p["sgu_ln_g"].reshape(1, d), p["sgu_ln_b"].reshape(1, d),
                        wm, bias_full, p["w_a"], seg_len, _pick_tile(rows, 512))

    chunks = SSD_CHUNKS_PER_STEP if length % (SSD_CHUNKS_PER_STEP * SSD_CHUNK) == 0 else 1
    q = chunks * SSD_CHUNK
    n_tiles = -(-length // q)
    padded = n_tiles * q
    valid_last = length - (n_tiles - 1) * q

    def pad_rows(arr):
        if padded == length:
            return arr
        arr = arr.reshape(n_streams, length, arr.shape[-1])
        arr = jnp.pad(arr, ((0, 0), (0, padded - length), (0, 0)))
        return arr.reshape(n_streams * padded, arr.shape[-1])

    gw = d_ssm // SSM_GROUPS
    a_row = (-jnp.exp(p["a_log"].astype(F32))).reshape(1, n_heads)
    d_full = jnp.repeat(p["d_skip"].astype(F32), SSM_HEADDIM).reshape(1, d_ssm)
    ssd = functools.partial(_ssd, pad_rows(xbc), pad_rows(dt), pad_rows(zs), a_row, d_full,
                            p["ssm_norm_g"].reshape(1, d_ssm),
                            h0.reshape(n_streams, SSM_GROUPS, gw, SSM_STATE),
                            n_streams, n_tiles, valid_last, chunks)
    if fuse_conv:
        yn, h_new = ssd()
        tiles_per_stream = length // tm
        c_new = conv_tail[tiles_per_stream - 1::tiles_per_stream,
                          CONV_PAD_ROWS - (CONV_WIDTH - 1):, :]
    else:
        yn, h_new, c_new = ssd(conv=(p["conv_w"], conv_b, c0))
    if padded != length:
        yn = yn.reshape(n_streams, padded, d_ssm)[:, :length].reshape(rows, d_ssm)
    h_new = h_new.reshape(n_streams, n_heads, SSM_HEADDIM, SSM_STATE)

    x1 = _outproj(yn, gaya, gates, x2, p["w_b"], p["w_o"], _pick_tile(rows, 256))
    y = _ffn(x1, p["norm_ffn_g"].reshape(1, d), p["w_gate"], p["w_up"], p["w_down"],
             g_final.reshape(1, d), _pick_tile(rows, 512), 512)
    return (y.reshape(n_streams, length, d), h_new, c_new,
            v_norm.reshape(n_streams, length, d))


def _layer_params(i, norm_mix_g, w_in, sgu_ln_g, sgu_ln_b, sgu_w, sgu_b, w_a, conv_w, conv_b,
                  dt_bias, a_log, d_skip, ssm_norm_g, w_b, w_o, norm_ffn_g, w_gate, w_up, w_down):
    d = w_in.shape[1]
    d_ssm = w_b.shape[1]
    conv_dim = conv_w.shape[2]
    n_heads = dt_bias.shape[1]
    s0 = 2 * d
    s1 = s0 + d_ssm
    s2 = s1 + conv_dim
    s3 = s2 + n_heads
    wi = w_in[i]
    return dict(
        norm_mix_g=norm_mix_g[i],
        w_uv=wi[:, :s0].astype(BF16), w_z=wi[:, s0:s1].astype(BF16),
        w_xbc=wi[:, s1:s2].astype(BF16), w_dt=wi[:, s2:s3].astype(BF16),
        w_gates=wi[:, s3:].astype(BF16),
        sgu_ln_g=sgu_ln_g[i], sgu_ln_b=sgu_ln_b[i], sgu_w=sgu_w[i], sgu_b=sgu_b[i],
        w_a=w_a[i].astype(BF16), conv_w=conv_w[i], conv_b=conv_b[i], dt_bias=dt_bias[i],
        a_log=a_log[i], d_skip=d_skip[i], ssm_norm_g=ssm_norm_g[i],
        w_b=w_b[i].astype(BF16), w_o=w_o[i].astype(BF16), norm_ffn_g=norm_ffn_g[i],
        w_gate=w_gate[i].astype(BF16), w_up=w_up[i].astype(BF16), w_down=w_down[i].astype(BF16),
    )


def kernel(x_prompt, x_sample, state_ssm, state_conv, norm_mix_g, w_in, sgu_ln_g, sgu_ln_b, sgu_w, sgu_b, w_a, conv_w, conv_b, dt_bias, a_log, d_skip, ssm_norm_g, w_b, w_o, norm_ffn_g, w_gate, w_up, w_down, norm_final_g):
    depth = w_in.shape[0]
    assert depth == 1, "multi-layer stacks need the final norm split from the FFN kernel"
    n_prompt = x_prompt.shape[0]
    n_heads = dt_bias.shape[1]
    conv_dim = conv_w.shape[2]
    p = _layer_params(0, norm_mix_g, w_in, sgu_ln_g, sgu_ln_b, sgu_w, sgu_b, w_a, conv_w, conv_b,
                      dt_bias, a_log, d_skip, ssm_norm_g, w_b, w_o, norm_ffn_g, w_gate, w_up,
                      w_down)
    h0_p = jnp.zeros((n_prompt, n_heads, SSM_HEADDIM, SSM_STATE), F32)
    c0_p = jnp.zeros((n_prompt, CONV_WIDTH - 1, conv_dim), F32)
    y_p, h_p, c_p, _ = _layer(x_prompt, h0_p, c0_p, p, norm_final_g)
    y_s, h_s, c_s, v_s = _layer(x_sample, state_ssm[0], state_conv[0], p, norm_final_g)
    return (y_p, y_s, h_p[None], c_p[None], h_s[None], c_s[None], v_s[None])
```

```python
import functools

import jax
import jax.numpy as jnp
from jax import lax
from jax.experimental import pallas as pl
from jax.experimental.pallas import tpu as pltpu

F32 = jnp.float32
BF16 = jnp.bfloat16

EPS = 1e-6
SGU_CHUNK = 128
SGU_GROUPS = 8
SSM_HEADDIM = 64
SSM_GROUPS = 8
SSM_STATE = 128
CONV_WIDTH = 4
SSD_CHUNK = 64
SSD_CHUNKS_PER_STEP = 2

V7X_VMEM_LIMIT_BYTES = 60 * 1024 * 1024
LANES = 128
CONV_PAD_ROWS = 8
PROJ_EPILOGUE_ROWS = 32


def _cparams(*sem):
    return pltpu.CompilerParams(dimension_semantics=sem, vmem_limit_bytes=V7X_VMEM_LIMIT_BYTES)


def _dot(a, b):
    return jnp.dot(a, b, preferred_element_type=F32)


def _dot_nt(a, b):
    return lax.dot_general(a, b, (((1,), (1,)), ((), ())), preferred_element_type=F32)


def _rms(x, g):
    return x * lax.rsqrt(jnp.mean(x * x, axis=-1, keepdims=True) + EPS) * g


def _split_bf16(x):
    hi = x.astype(BF16)
    lo = (x - hi.astype(F32)).astype(BF16)
    return hi, lo


def _sigmoid(x):
    return 0.5 * jnp.tanh(0.5 * x) + 0.5


def _act_gelu(x):
    return jax.nn.gelu(x, approximate=True)


def _act_silu(x):
    return x * _sigmoid(x)


def _act_softplus(x):
    return jnp.maximum(x, 0.0) + jnp.log1p(jnp.exp(-jnp.abs(x)))


def _act_none(x):
    return x


def _rmsnorm_kernel(x_ref, g_ref, o_ref):
    o_ref[...] = _rms(x_ref[...], g_ref[...]).astype(o_ref.dtype)


def _rmsnorm_cast(x, g, tm):
    rows, d = x.shape
    return pl.pallas_call(
        _rmsnorm_kernel,
        grid=(rows // tm,),
        in_specs=[pl.BlockSpec((tm, d), lambda i: (i, 0)), pl.BlockSpec((1, d), lambda i: (0, 0))],
        out_specs=pl.BlockSpec((tm, d), lambda i: (i, 0)),
        out_shape=jax.ShapeDtypeStruct((rows, d), BF16),
        compiler_params=_cparams("arbitrary"),
        name="rmsnorm",
    )(x, g)


def _by_parity(n, body, acc0_ref, acc1_ref):
    @pl.when(n % 2 == 0)
    def _():
        body(acc1_ref, acc0_ref)

    @pl.when(n % 2 == 1)
    def _():
        body(acc0_ref, acc1_ref)


def _proj_kernel(*refs, act, has_bias):
    if has_bias:
        xn_ref, w_ref, b_ref, o_ref, acc0_ref, acc1_ref = refs
    else:
        xn_ref, w_ref, o_ref, acc0_ref, acc1_ref = refs
    n = pl.program_id(0)

    @pl.when(n == 0)
    def _():
        acc1_ref[...] = jnp.zeros_like(acc1_ref)

    tm = o_ref.shape[0]
    ec = min(tm, PROJ_EPILOGUE_ROWS)

    def body(prev_ref, next_ref):
        outs = []
        for r in range(0, tm, ec):
            prev = prev_ref[r:r + ec, :]
            if has_bias:
                prev = prev + b_ref[...]
            outs.append(act(prev).astype(o_ref.dtype))
        res = _dot(xn_ref[...], w_ref[...])
        for i, r in enumerate(range(0, tm, ec)):
            o_ref[r:r + ec, :] = outs[i]
        next_ref[...] = res

    _by_parity(n, body, acc0_ref, acc1_ref)


def _tile_maps(n_row, n_col):
    last = n_row * n_col - 1
    cur = lambda n: jnp.minimum(n, last)
    prv = lambda n: jnp.maximum(n - 1, 0)
    return cur, prv


def _proj(xn, w, bias, act, out_dtype, tm, tn):
    rows, d = xn.shape
    n_out = w.shape[1]
    n_row, n_col = rows // tm, n_out // tn
    cur, prv = _tile_maps(n_row, n_col)
    in_specs = [
        pl.BlockSpec((tm, d), lambda n: (cur(n) // n_col, 0)),
        pl.BlockSpec((d, tn), lambda n: (0, cur(n) % n_col)),
    ]
    args = [xn, w]
    if bias is not None:
        in_specs.append(pl.BlockSpec((1, tn), lambda n: (0, prv(n) % n_col)))
        args.append(bias)
    return pl.pallas_call(
        functools.partial(_proj_kernel, act=act, has_bias=bias is not None),
        grid=(n_row * n_col + 1,),
        in_specs=in_specs,
        out_specs=pl.BlockSpec((tm, tn), lambda n: (prv(n) // n_col, prv(n) % n_col)),
        out_shape=jax.ShapeDtypeStruct((rows, n_out), out_dtype),
        scratch_shapes=[pltpu.VMEM((tm, tn), F32), pltpu.VMEM((tm, tn), F32)],
        compiler_params=_cparams("arbitrary"),
        name="proj",
    )(*args)


def _proj_conv_kernel(xn_ref, w_ref, cw_ref, cb_ref, c0_ref, o_ref, tail_ref, acc0_ref, acc1_ref,
                      hist_ref, *, n_col, tiles_per_stream):
    n = pl.program_id(0)
    tm = o_ref.shape[0]
    pad = CONV_PAD_ROWS

    @pl.when(n == 0)
    def _():
        acc1_ref[...] = jnp.zeros_like(acc1_ref)
        hist_ref[...] = jnp.zeros_like(hist_ref)

    prv = jnp.maximum(n - 1, 0)
    jp = prv % n_col
    stream_start = ((prv // n_col) % tiles_per_stream) == 0

    ec = min(tm, PROJ_EPILOGUE_ROWS)

    def body(prev_ref, next_ref):
        prev_ref[0:pad, :] = jnp.where(stream_start, c0_ref[0], hist_ref[jp])
        last = prev_ref[tm:tm + pad, :]
        hist_ref[jp] = last
        tail_ref[0] = last
        outs = []
        for r in range(0, tm, ec):
            raw = prev_ref[r:r + pad + ec, :]
            conv = raw * cw_ref[CONV_WIDTH - 1:CONV_WIDTH, :] + cb_ref[...]
            for back in range(1, CONV_WIDTH):
                k = CONV_WIDTH - 1 - back
                conv = conv + pltpu.roll(raw, back, 0) * cw_ref[k:k + 1, :]
            outs.append(_act_silu(conv[pad:pad + ec, :]).astype(o_ref.dtype))
        res = _dot(xn_ref[...], w_ref[...])
        for i, r in enumerate(range(0, tm, ec)):
            o_ref[r:r + ec, :] = outs[i]
        next_ref[pad:pad + tm, :] = res

    _by_parity(n, body, acc0_ref, acc1_ref)


def _proj_conv(xn, w, conv_w, conv_b, c0_padded, n_streams, tm, tn):
    rows, d = xn.shape
    n_out = w.shape[1]
    n_row, n_col = rows // tm, n_out // tn
    tiles_per_stream = n_row // n_streams
    cur, prv = _tile_maps(n_row, n_col)
    stream = lambda n: prv(n) // n_col // tiles_per_stream
    return pl.pallas_call(
        functools.partial(_proj_conv_kernel, n_col=n_col, tiles_per_stream=tiles_per_stream),
        grid=(n_row * n_col + 1,),
        in_specs=[
            pl.BlockSpec((tm, d), lambda n: (cur(n) // n_col, 0)),
            pl.BlockSpec((d, tn), lambda n: (0, cur(n) % n_col)),
            pl.BlockSpec((CONV_WIDTH, tn), lambda n: (0, prv(n) % n_col)),
            pl.BlockSpec((1, tn), lambda n: (0, prv(n) % n_col)),
            pl.BlockSpec((1, CONV_PAD_ROWS, tn), lambda n: (stream(n), 0, prv(n) % n_col)),
        ],
        out_specs=[
            pl.BlockSpec((tm, tn), lambda n: (prv(n) // n_col, prv(n) % n_col)),
            pl.BlockSpec((1, CONV_PAD_ROWS, tn), lambda n: (prv(n) // n_col, 0, prv(n) % n_col)),
        ],
        out_shape=[
            jax.ShapeDtypeStruct((rows, n_out), BF16),
            jax.ShapeDtypeStruct((n_row, CONV_PAD_ROWS, n_out), F32),
        ],
        scratch_shapes=[
            pltpu.VMEM((CONV_PAD_ROWS + tm, tn), F32),
            pltpu.VMEM((CONV_PAD_ROWS + tm, tn), F32),
            pltpu.VMEM((n_col, CONV_PAD_ROWS, tn), F32),
        ],
        compiler_params=_cparams("arbitrary"),
        name="proj_conv",
    )(xn, w, conv_w, conv_b, c0_padded)


def _sgu_kernel(*refs, seg_len, emit_v):
    if emit_v:
        (uv_ref, ga_ref, lng_ref, lnb_ref, wm_ref, bias_ref, wa_ref, o_ref, v_ref,
         gated0_ref, gated1_ref) = refs
    else:
        (uv_ref, ga_ref, lng_ref, lnb_ref, wm_ref, bias_ref, wa_ref, o_ref,
         gated0_ref, gated1_ref) = refs
    n = pl.program_id(0)
    tm = uv_ref.shape[0]
    d = o_ref.shape[1]
    gd = d // SGU_GROUPS

    @pl.when(n == 0)
    def _():
        gated1_ref[...] = jnp.zeros_like(gated1_ref)

    row = lax.broadcasted_iota(jnp.int32, (SGU_CHUNK, SGU_CHUNK), 0)
    col = lax.broadcasted_iota(jnp.int32, (SGU_CHUNK, SGU_CHUNK), 1)
    mask = (col <= row) & ((row // seg_len) == (col // seg_len))

    def body(prev_ref, next_ref):
        ya = _dot(prev_ref[...], wa_ref[...])
        out = (ga_ref[...].astype(F32) * ya).astype(o_ref.dtype)
        ws = [jnp.where(mask, wm_ref[g], 0.0).astype(BF16) for g in range(SGU_GROUPS)]
        gated_rows = []
        v_rows = []
        for c in range(tm // SGU_CHUNK):
            rs = slice(c * SGU_CHUNK, (c + 1) * SGU_CHUNK)
            v = uv_ref[rs, d:].astype(F32)
            mu = jnp.mean(v, axis=-1, keepdims=True)
            vc = v - mu
            var = jnp.mean(vc * vc, axis=-1, keepdims=True)
            vn = vc * lax.rsqrt(var + EPS) * lng_ref[...] + lnb_ref[...]
            v_rows.append(vn)
            vb = vn.astype(BF16)
            cols = []
            for g in range(SGU_GROUPS):
                cs = slice(g * gd, (g + 1) * gd)
                mixed = _dot(ws[g], vb[:, cs]) + bias_ref[:, cs]
                cols.append((uv_ref[rs, cs].astype(F32) * mixed).astype(BF16))
            gated_rows.append(jnp.concatenate(cols, axis=1))
        o_ref[...] = out
        for c in range(tm // SGU_CHUNK):
            rs = slice(c * SGU_CHUNK, (c + 1) * SGU_CHUNK)
            next_ref[rs, :] = gated_rows[c]
            if emit_v:
                v_ref[rs, :] = v_rows[c]

    _by_parity(n, body, gated0_ref, gated1_ref)


def _sgu(uv, gates, ln_g, ln_b, wm, bias_full, w_a, seg_len, tm, emit_v):
    rows = uv.shape[0]
    d = w_a.shape[0]
    n_row = rows // tm
    cur, prv = _tile_maps(n_row, 1)
    const2 = lambda n: (0, 0)
    out_specs = [pl.BlockSpec((tm, d), lambda n: (prv(n), 0))]
    out_shape = [jax.ShapeDtypeStruct((rows, d), BF16)]
    if emit_v:
        out_specs.append(pl.BlockSpec((tm, d), lambda n: (cur(n), 0)))
        out_shape.append(jax.ShapeDtypeStruct((rows, d), F32))
    return pl.pallas_call(
        functools.partial(_sgu_kernel, seg_len=seg_len, emit_v=emit_v),
        grid=(n_row + 1,),
        in_specs=[
            pl.BlockSpec((tm, 2 * d), lambda n: (cur(n), 0)),
            pl.BlockSpec((tm, d), lambda n: (prv(n), 0)),
            pl.BlockSpec((1, d), const2),
            pl.BlockSpec((1, d), const2),
            pl.BlockSpec((SGU_GROUPS, SGU_CHUNK, SGU_CHUNK), lambda n: (0, 0, 0)),
            pl.BlockSpec((SGU_CHUNK, d), const2),
            pl.BlockSpec((d, d), const2),
        ],
        out_specs=out_specs,
        out_shape=out_shape,
        scratch_shapes=[pltpu.VMEM((tm, d), BF16), pltpu.VMEM((tm, d), BF16)],
        compiler_params=_cparams("arbitrary"),
        name="sgu",
    )(uv, gates, ln_g, ln_b, wm, bias_full, w_a)


def _ssd_kernel(*refs, n_tiles, valid_last, conv_in_kernel, chunks):
    if conv_in_kernel:
        (xbc_ref, dt_ref, z_ref, a_ref, d_ref, ng_ref, h0_ref, cw_ref, cb_ref, c0_ref,
         y_ref, hout_ref, cout_ref, ht_ref, cbuf_ref) = refs
    else:
        xbc_ref, dt_ref, z_ref, a_ref, d_ref, ng_ref, h0_ref, y_ref, hout_ref, ht_ref = refs
    t = pl.program_id(1)
    q = SSD_CHUNK
    d_ssm = y_ref.shape[1]
    n_heads = d_ssm // SSM_HEADDIM
    gw = d_ssm // SSM_GROUPS
    pair = 2 * SSM_HEADDIM
    n_bc = SSM_GROUPS * SSM_STATE
    assert pair == LANES and q == SSM_HEADDIM

    @pl.when(t == 0)
    def _init():
        for g in range(SSM_GROUPS):
            ht_ref[g] = h0_ref[0, g].T
        if conv_in_kernel:
            cbuf_ref[0:CONV_PAD_ROWS, :] = jnp.zeros((CONV_PAD_ROWS, cbuf_ref.shape[1]), F32)
            cbuf_ref[CONV_PAD_ROWS - (CONV_WIDTH - 1):CONV_PAD_ROWS, :] = c0_ref[0]

    tq = chunks * q
    n_pairs = gw // pair
    if conv_in_kernel:
        cbuf_ref[CONV_PAD_ROWS:CONV_PAD_ROWS + tq, :] = xbc_ref[...].astype(F32)
        conv = cb_ref[...]
        for k in range(CONV_WIDTH):
            r0 = CONV_PAD_ROWS - (CONV_WIDTH - 1) + k
            conv = conv + cbuf_ref[r0:r0 + tq, :] * cw_ref[k:k + 1, :]

        @pl.when(t == n_tiles - 1)
        def _conv_out():
            r0 = CONV_PAD_ROWS + valid_last - (CONV_WIDTH - 1)
            cout_ref[0] = cbuf_ref[r0:r0 + CONV_WIDTH - 1, :]

        cbuf_ref[0:CONV_PAD_ROWS, :] = cbuf_ref[tq:tq + CONV_PAD_ROWS, :]
        xc_all = _act_silu(conv).astype(BF16)
    else:
        xc_all = xbc_ref[...]

    row = lax.broadcasted_iota(jnp.int32, (q, gw), 0)
    pos = lax.broadcasted_iota(jnp.int32, (q, gw), 1) & (SSM_HEADDIM - 1)
    on_diag = row == pos
    causal = row >= pos
    hrow = lax.broadcasted_iota(jnp.int32, (n_heads, d_ssm), 0)
    hlane = lax.broadcasted_iota(jnp.int32, (n_heads, d_ssm), 1)
    expand = (hrow == (hlane // SSM_HEADDIM)).astype(BF16)
    r2 = lax.broadcasted_iota(jnp.int32, (q, q), 0)
    c2 = lax.broadcasted_iota(jnp.int32, (q, q), 1)
    lower = (c2 <= r2).astype(BF16)
    eye = (lax.broadcasted_iota(jnp.int32, (SSM_STATE, SSM_STATE), 0)
           == lax.broadcasted_iota(jnp.int32, (SSM_STATE, SSM_STATE), 1)).astype(BF16)
    first_head = lax.broadcasted_iota(jnp.int32, (q, pair), 1) < SSM_HEADDIM

    state = []
    for g in range(SSM_GROUPS):
        htg = ht_ref[g]
        state.append([htg[:, j * pair:(j + 1) * pair] for j in range(n_pairs)])
    y_out = []

    for c in range(chunks):
        rs = slice(c * q, (c + 1) * q)
        dt = dt_ref[rs, :]
        dt_bf = dt.astype(BF16)
        da_hi, da_lo = _split_bf16(dt * a_ref[...])
        cum_hi, cum_lo = _split_bf16(_dot(lower, da_hi) + _dot(lower, da_lo))

        y_groups = []
        for g in range(SSM_GROUPS):
            gs = slice(g * gw, (g + 1) * gw)
            ns = slice(d_ssm + g * SSM_STATE, d_ssm + (g + 1) * SSM_STATE)
            colb = _dot(cum_hi, expand[:, gs]) + _dot(cum_lo, expand[:, gs])
            rowb = jnp.sum(jnp.where(on_diag, colb, 0.0), axis=0, keepdims=True)
            mw = jnp.where(causal, jnp.exp(jnp.minimum(colb - rowb, 0.0)), 0.0)
            expc = jnp.exp(colb)
            decl = expc[q - 1:q, :]
            tailw = jnp.exp(jnp.minimum(colb[q - 1:q, :] - rowb, 0.0))
            xs = xc_all[rs, gs].astype(F32)
            xdt = (xs * _dot(dt_bf, expand[:, gs])).astype(BF16)

            bg = xc_all[rs, ns]
            cg = xc_all[rs, n_bc + ns.start:n_bc + ns.stop]
            bb = jnp.concatenate([bg, bg], axis=0)
            cb2 = _dot_nt(cg, bb)
            bt2 = _dot_nt(eye, bb)
            yst = _dot(cg, jnp.concatenate(state[g], axis=1).astype(BF16))
            ys = []
            for j in range(n_pairs):
                js = slice(j * pair, (j + 1) * pair)
                m2 = (cb2 * mw[:, js]).astype(BF16)
                xd2 = xdt[:, js]
                zero = jnp.zeros_like(xd2)
                xbd = jnp.concatenate([jnp.where(first_head, xd2, zero),
                                       jnp.where(first_head, zero, xd2)], axis=0)
                ys.append(_dot(m2, xbd) + expc[:, js] * yst[:, js]
                          + d_ref[:, g * gw + j * pair:g * gw + (j + 1) * pair] * xs[:, js])
                btw = (bt2 * tailw[:, js]).astype(BF16)
                state[g][j] = state[g][j] * decl[:, js] + _dot(btw, xbd)
            yg = jnp.concatenate(ys, axis=1) * z_ref[rs, gs].astype(F32)
            y_groups.append(_rms(yg, ng_ref[:, gs]).astype(y_ref.dtype))
        y_out.append(y_groups)

    for c in range(chunks):
        for g in range(SSM_GROUPS):
            y_ref[c * q:(c + 1) * q, g * gw:(g + 1) * gw] = y_out[c][g]
    for g in range(SSM_GROUPS):
        ht_ref[g] = jnp.concatenate(state[g], axis=1)

    @pl.when(t == n_tiles - 1)
    def _state_out():
        for g in range(SSM_GROUPS):
            hout_ref[0, g] = ht_ref[g].T


def _ssd(xbc, dt, zs, a_row, d_full, norm_g, h0, n_streams, n_tiles, valid_last, chunks,
         conv=None):
    rows, conv_dim = xbc.shape
    d_ssm = zs.shape[1]
    n_heads = dt.shape[1]
    gw = d_ssm // SSM_GROUPS
    q = chunks * SSD_CHUNK
    rb = lambda s, t: (s * n_tiles + t, 0)
    const2 = lambda s, t: (0, 0)
    per_stream3 = lambda s, t: (s, 0, 0)
    state_spec = pl.BlockSpec((1, SSM_GROUPS, gw, SSM_STATE), lambda s, t: (s, 0, 0, 0))
    in_specs = [
        pl.BlockSpec((q, conv_dim), rb),
        pl.BlockSpec((q, n_heads), rb),
        pl.BlockSpec((q, d_ssm), rb),
        pl.BlockSpec((1, n_heads), const2),
        pl.BlockSpec((1, d_ssm), const2),
        pl.BlockSpec((1, d_ssm), const2),
        state_spec,
    ]
    args = [xbc, dt, zs, a_row, d_full, norm_g, h0]
    out_specs = [pl.BlockSpec((q, d_ssm), rb), state_spec]
    out_shape = [
        jax.ShapeDtypeStruct((rows, d_ssm), BF16),
        jax.ShapeDtypeStruct((n_streams, SSM_GROUPS, gw, SSM_STATE), F32),
    ]
    scratch = [pltpu.VMEM((SSM_GROUPS, SSM_STATE, gw), F32)]
    if conv is not None:
        in_specs += [
            pl.BlockSpec((CONV_WIDTH, conv_dim), const2),
            pl.BlockSpec((1, conv_dim), const2),
            pl.BlockSpec((1, CONV_WIDTH - 1, conv_dim), per_stream3),
        ]
        args += list(conv)
        out_specs.append(pl.BlockSpec((1, CONV_WIDTH - 1, conv_dim), per_stream3))
        out_shape.append(jax.ShapeDtypeStruct((n_streams, CONV_WIDTH - 1, conv_dim), F32))
        scratch.append(pltpu.VMEM((CONV_PAD_ROWS + q, conv_dim), F32))
    return pl.pallas_call(
        functools.partial(_ssd_kernel, n_tiles=n_tiles, valid_last=valid_last,
                          conv_in_kernel=conv is not None, chunks=chunks),
        grid=(n_streams, n_tiles),
        in_specs=in_specs,
        out_specs=out_specs,
        out_shape=out_shape,
        scratch_shapes=scratch,
        compiler_params=_cparams("arbitrary", "arbitrary"),
        name="ssd",
    )(*args)


def _outproj_kernel(yn_ref, gaya_ref, gb_ref, x_ref, wb_ref, wo_ref, o_ref):
    yb = _dot(yn_ref[...], wb_ref[...])
    merged = gaya_ref[...].astype(F32) + gb_ref[...].astype(F32) * yb
    o_ref[...] = x_ref[...] + _dot(merged.astype(BF16), wo_ref[...])


def _outproj(yn, gaya, gates, x, w_b, w_o, tm):
    rows, d = x.shape
    d_ssm = yn.shape[1]
    return pl.pallas_call(
        _outproj_kernel,
        grid=(rows // tm,),
        in_specs=[
            pl.BlockSpec((tm, d_ssm), lambda i: (i, 0)),
            pl.BlockSpec((tm, d), lambda i: (i, 0)),
            pl.BlockSpec((tm, d), lambda i: (i, 1)),
            pl.BlockSpec((tm, d), lambda i: (i, 0)),
            pl.BlockSpec((d_ssm, d), lambda i: (0, 0), pipeline_mode=pl.Buffered(1)),
            pl.BlockSpec((d, d), lambda i: (0, 0), pipeline_mode=pl.Buffered(1)),
        ],
        out_specs=pl.BlockSpec((tm, d), lambda i: (i, 0)),
        out_shape=jax.ShapeDtypeStruct((rows, d), F32),
        compiler_params=_cparams("arbitrary"),
        name="outproj",
    )(yn, gaya, gates, x, w_b, w_o)


def _ffn_kernel(x_ref, g_ref, wg_ref, wu_ref, wd_ref, gf_ref, o_ref, hf_ref, acc_ref, *, n_ff):
    j = pl.program_id(1)

    @pl.when(j == 0)
    def _():
        hf_ref[...] = _rms(x_ref[...], g_ref[...]).astype(BF16)
        acc_ref[...] = jnp.zeros_like(acc_ref)

    hf = hf_ref[...]
    gate = _dot(hf, wg_ref[...])
    up = _dot(hf, wu_ref[...])
    mid = (gate * jax.nn.sigmoid(gate) * up).astype(BF16)
    acc_ref[...] += _dot(mid, wd_ref[...])

    @pl.when(j == n_ff - 1)
    def _():
        o_ref[...] = _rms(x_ref[...] + acc_ref[...], gf_ref[...])


def _ffn(x, g, w_gate, w_up, w_down, g_final, tm, tf):
    rows, d = x.shape
    d_ff = w_gate.shape[1]
    n_ff = d_ff // tf
    return pl.pallas_call(
        functools.partial(_ffn_kernel, n_ff=n_ff),
        grid=(rows // tm, n_ff),
        in_specs=[
            pl.BlockSpec((tm, d), lambda i, j: (i, 0)),
            pl.BlockSpec((1, d), lambda i, j: (0, 0)),
            pl.BlockSpec((d, tf), lambda i, j: (0, j)),
            pl.BlockSpec((d, tf), lambda i, j: (0, j)),
            pl.BlockSpec((tf, d), lambda i, j: (j, 0)),
            pl.BlockSpec((1, d), lambda i, j: (0, 0)),
        ],
        out_specs=pl.BlockSpec((tm, d), lambda i, j: (i, 0)),
        out_shape=jax.ShapeDtypeStruct((rows, d), F32),
        scratch_shapes=[pltpu.VMEM((tm, d), BF16), pltpu.VMEM((tm, d), F32)],
        compiler_params=_cparams("arbitrary", "arbitrary"),
        name="ffn",
    )(x, g, w_gate, w_up, w_down, g_final)


def _pick_tile(rows, preferred):
    tile = min(rows, preferred)
    assert rows % tile == 0, (rows, tile)
    return tile


def _layer(x, h0, c0, p, g_final, want_v):
    n_streams, length, d = x.shape
    rows = n_streams * length
    x2 = x.reshape(rows, d)
    d_ssm = p["w_b"].shape[0]
    n_heads = p["dt_bias"].shape[0]
    conv_dim = p["conv_w"].shape[1]
    tm = _pick_tile(rows, 1024)
    tn = 1024
    tn_wide = 2048
    xn = _rmsnorm_cast(x2, p["norm_mix_g"].reshape(1, d), tm)
    uv = _proj(xn, p["w_uv"], None, _act_gelu, BF16, tm, tn)
    zs = _proj(xn, p["w_z"], None, _act_silu, BF16, tm, tn_wide)
    dt = _proj(xn, p["w_dt"], p["dt_bias"].reshape(1, n_heads), _act_softplus, F32, tm, n_heads)
    gates = _proj(xn, p["w_gates"], None, _sigmoid, BF16, tm, tn_wide)
    conv_b = p["conv_b"].reshape(1, conv_dim)
    fuse_conv = length % tm == 0
    if fuse_conv:
        c0_padded = jnp.pad(c0, ((0, 0), (CONV_PAD_ROWS - (CONV_WIDTH - 1), 0), (0, 0)))
        xbc, conv_tail = _proj_conv(xn, p["w_xbc"], p["conv_w"], conv_b, c0_padded, n_streams,
                                    tm, tn)
    else:
        xbc = _proj(xn, p["w_xbc"], None, _act_none, BF16, tm, tn)

    seg_len = min(length, SGU_CHUNK)
    assert SGU_CHUNK % seg_len == 0 and length % seg_len == 0
    rep = SGU_CHUNK // seg_len
    wm = jnp.tile(p["sgu_w"][:, :seg_len, :seg_len], (1, rep, rep))
    bias_full = jnp.repeat(jnp.tile(p["sgu_b"][:, :seg_len], (1, rep)).T, d // SGU_GROUPS, axis=1)
    sgu_out = _sgu(uv, gates, p["sgu_ln_g"].reshape(1, d), p["sgu_ln_b"].reshape(1, d),
                   wm, bias_full, p["w_a"], seg_len, _pick_tile(rows, 512), want_v)
    gaya = sgu_out[0]
    v_norm = sgu_out[1].reshape(n_streams, length, d) if want_v else None

    chunks = SSD_CHUNKS_PER_STEP if length % (SSD_CHUNKS_PER_STEP * SSD_CHUNK) == 0 else 1
    q = chunks * SSD_CHUNK
    n_tiles = -(-length // q)
    padded = n_tiles * q
    valid_last = length - (n_tiles - 1) * q

    def pad_rows(arr):
        if padded == length:
            return arr
        arr = arr.reshape(n_streams, length, arr.shape[-1])
        arr = jnp.pad(arr, ((0, 0), (0, padded - length), (0, 0)))
        return arr.reshape(n_streams * padded, arr.shape[-1])

    gw = d_ssm // SSM_GROUPS
    a_row = (-jnp.exp(p["a_log"].astype(F32))).reshape(1, n_heads)
    d_full = jnp.repeat(p["d_skip"].astype(F32), SSM_HEADDIM).reshape(1, d_ssm)
    ssd = functools.partial(_ssd, pad_rows(xbc), pad_rows(dt), pad_rows(zs), a_row, d_full,
                            p["ssm_norm_g"].reshape(1, d_ssm),
                            h0.reshape(n_streams, SSM_GROUPS, gw, SSM_STATE),
                            n_streams, n_tiles, valid_last, chunks)
    if fuse_conv:
        yn, h_new = ssd()
        tiles_per_stream = length // tm
        c_new = conv_tail[tiles_per_stream - 1::tiles_per_stream,
                          CONV_PAD_ROWS - (CONV_WIDTH - 1):, :]
    else:
        yn, h_new, c_new = ssd(conv=(p["conv_w"], conv_b, c0))
    if padded != length:
        yn = yn.reshape(n_streams, padded, d_ssm)[:, :length].reshape(rows, d_ssm)
    h_new = h_new.reshape(n_streams, n_heads, SSM_HEADDIM, SSM_STATE)

    x1 = _outproj(yn, gaya, gates, x2, p["w_b"], p["w_o"], _pick_tile(rows, 256))
    y = _ffn(x1, p["norm_ffn_g"].reshape(1, d), p["w_gate"], p["w_up"], p["w_down"],
             g_final.reshape(1, d), _pick_tile(rows, 512), 512)
    return y.reshape(n_streams, length, d), h_new, c_new, v_norm


def _layer_params(i, norm_mix_g, w_in, sgu_ln_g, sgu_ln_b, sgu_w, sgu_b, w_a, conv_w, conv_b,
                  dt_bias, a_log, d_skip, ssm_norm_g, w_b, w_o, norm_ffn_g, w_gate, w_up, w_down):
    d = w_in.shape[1]
    d_ssm = w_b.shape[1]
    conv_dim = conv_w.shape[2]
    n_heads = dt_bias.shape[1]
    s0 = 2 * d
    s1 = s0 + d_ssm
    s2 = s1 + conv_dim
    s3 = s2 + n_heads
    wi = w_in[i]
    return dict(
        norm_mix_g=norm_mix_g[i],
        w_uv=wi[:, :s0].astype(BF16), w_z=wi[:, s0:s1].astype(BF16),
        w_xbc=wi[:, s1:s2].astype(BF16), w_dt=wi[:, s2:s3].astype(BF16),
        w_gates=wi[:, s3:].astype(BF16),
        sgu_ln_g=sgu_ln_g[i], sgu_ln_b=sgu_ln_b[i], sgu_w=sgu_w[i], sgu_b=sgu_b[i],
        w_a=w_a[i].astype(BF16), conv_w=conv_w[i], conv_b=conv_b[i], dt_bias=dt_bias[i],
        a_log=a_log[i], d_skip=d_skip[i], ssm_norm_g=ssm_norm_g[i],
        w_b=w_b[i].astype(BF16), w_o=w_o[i].astype(BF16), norm_ffn_g=norm_ffn_g[i],
        w_gate=w_gate[i].astype(BF16), w_up=w_up[i].astype(BF16), w_down=w_down[i].astype(BF16),
    )


def kernel(x_prompt, x_sample, state_ssm, state_conv, norm_mix_g, w_in, sgu_ln_g, sgu_ln_b, sgu_w, sgu_b, w_a, conv_w, conv_b, dt_bias, a_log, d_skip, ssm_norm_g, w_b, w_o, norm_ffn_g, w_gate, w_up, w_down, norm_final_g):
    depth = w_in.shape[0]
    assert depth == 1, "multi-layer stacks need the final norm split from the FFN kernel"
    n_prompt = x_prompt.shape[0]
    n_heads = dt_bias.shape[1]
    conv_dim = conv_w.shape[2]
    p = _layer_params(0, norm_mix_g, w_in, sgu_ln_g, sgu_ln_b, sgu_w, sgu_b, w_a, conv_w, conv_b,
                      dt_bias, a_log, d_skip, ssm_norm_g, w_b, w_o, norm_ffn_g, w_gate, w_up,
                      w_down)
    h0_p = jnp.zeros((n_prompt, n_heads, SSM_HEADDIM, SSM_STATE), F32)
    c0_p = jnp.zeros((n_prompt, CONV_WIDTH - 1, conv_dim), F32)
    y_p, h_p, c_p, _ = _layer(x_prompt, h0_p, c0_p, p, norm_final_g, want_v=False)
    y_s, h_s, c_s, v_s = _layer(x_sample, state_ssm[0], state_conv[0], p, norm_final_g,
                                want_v=True)
    return (y_p, y_s, h_p[None], c_p[None], h_s[None], c_s[None], v_s[None])
```

```python
import functools

import jax
import jax.numpy as jnp
from jax import lax
from jax.experimental import pallas as pl
from jax.experimental.pallas import tpu as pltpu

F32 = jnp.float32
BF16 = jnp.bfloat16

EPS = 1e-6
SGU_CHUNK = 128
SGU_GROUPS = 8
SSM_HEADDIM = 64
SSM_GROUPS = 8
SSM_STATE = 128
CONV_WIDTH = 4
SSD_CHUNK = 64
SSD_CHUNKS_PER_STEP = 2
SSD_TERMS_AHEAD = 1

V7X_VMEM_LIMIT_BYTES = 56 * 1024 * 1024
LANES = 128
CONV_PAD_ROWS = 8
PROJ_EPILOGUE_ROWS = 32


def _cparams(*sem):
    return pltpu.CompilerParams(dimension_semantics=sem, vmem_limit_bytes=V7X_VMEM_LIMIT_BYTES)


def _dot(a, b):
    return jnp.dot(a, b, preferred_element_type=F32)


def _dot_nt(a, b):
    return lax.dot_general(a, b, (((1,), (1,)), ((), ())), preferred_element_type=F32)


def _rms(x, g):
    return x * lax.rsqrt(jnp.mean(x * x, axis=-1, keepdims=True) + EPS) * g


def _split_bf16(x):
    hi = x.astype(BF16)
    lo = (x - hi.astype(F32)).astype(BF16)
    return hi, lo


def _sigmoid(x):
    return 0.5 * jnp.tanh(0.5 * x) + 0.5


def _act_gelu(x):
    return jax.nn.gelu(x, approximate=True)


def _act_silu(x):
    return x * _sigmoid(x)


def _act_softplus(x):
    return jnp.maximum(x, 0.0) + jnp.log1p(jnp.exp(-jnp.abs(x)))


def _act_none(x):
    return x


def _rmsnorm_kernel(x_ref, g_ref, o_ref):
    o_ref[...] = _rms(x_ref[...], g_ref[...]).astype(o_ref.dtype)


def _rmsnorm_cast(x, g, tm):
    rows, d = x.shape
    return pl.pallas_call(
        _rmsnorm_kernel,
        grid=(rows // tm,),
        in_specs=[pl.BlockSpec((tm, d), lambda i: (i, 0)), pl.BlockSpec((1, d), lambda i: (0, 0))],
        out_specs=pl.BlockSpec((tm, d), lambda i: (i, 0)),
        out_shape=jax.ShapeDtypeStruct((rows, d), BF16),
        compiler_params=_cparams("arbitrary"),
        name="rmsnorm",
    )(x, g)


def _by_parity(n, body, acc0_ref, acc1_ref):
    @pl.when(n % 2 == 0)
    def _():
        body(acc1_ref, acc0_ref)

    @pl.when(n % 2 == 1)
    def _():
        body(acc0_ref, acc1_ref)


def _proj_kernel(*refs, act, has_bias):
    if has_bias:
        xn_ref, w_ref, b_ref, o_ref, acc0_ref, acc1_ref = refs
    else:
        xn_ref, w_ref, o_ref, acc0_ref, acc1_ref = refs
    n = pl.program_id(0)

    @pl.when(n == 0)
    def _():
        acc1_ref[...] = jnp.zeros_like(acc1_ref)

    tm = o_ref.shape[0]
    ec = min(tm, PROJ_EPILOGUE_ROWS)

    def body(prev_ref, next_ref):
        outs = []
        for r in range(0, tm, ec):
            prev = prev_ref[r:r + ec, :]
            if has_bias:
                prev = prev + b_ref[...]
            outs.append(act(prev).astype(o_ref.dtype))
        res = _dot(xn_ref[...], w_ref[...])
        for i, r in enumerate(range(0, tm, ec)):
            o_ref[r:r + ec, :] = outs[i]
        next_ref[...] = res

    _by_parity(n, body, acc0_ref, acc1_ref)


def _tile_maps(n_row, n_col):
    last = n_row * n_col - 1
    cur = lambda n: jnp.minimum(n, last)
    prv = lambda n: jnp.maximum(n - 1, 0)
    return cur, prv


def _proj(xn, w, bias, act, out_dtype, tm, tn):
    rows, d = xn.shape
    n_out = w.shape[1]
    n_row, n_col = rows // tm, n_out // tn
    cur, prv = _tile_maps(n_row, n_col)
    in_specs = [
        pl.BlockSpec((tm, d), lambda n: (cur(n) // n_col, 0)),
        pl.BlockSpec((d, tn), lambda n: (0, cur(n) % n_col)),
    ]
    args = [xn, w]
    if bias is not None:
        in_specs.append(pl.BlockSpec((1, tn), lambda n: (0, prv(n) % n_col)))
        args.append(bias)
    return pl.pallas_call(
        functools.partial(_proj_kernel, act=act, has_bias=bias is not None),
        grid=(n_row * n_col + 1,),
        in_specs=in_specs,
        out_specs=pl.BlockSpec((tm, tn), lambda n: (prv(n) // n_col, prv(n) % n_col)),
        out_shape=jax.ShapeDtypeStruct((rows, n_out), out_dtype),
        scratch_shapes=[pltpu.VMEM((tm, tn), F32), pltpu.VMEM((tm, tn), F32)],
        compiler_params=_cparams("arbitrary"),
        name="proj",
    )(*args)


def _proj_conv_kernel(xn_ref, w_ref, cw_ref, cb_ref, c0_ref, o_ref, tail_ref, acc0_ref, acc1_ref,
                      hist_ref, *, n_col, tiles_per_stream):
    n = pl.program_id(0)
    tm = o_ref.shape[0]
    pad = CONV_PAD_ROWS

    @pl.when(n == 0)
    def _():
        acc1_ref[...] = jnp.zeros_like(acc1_ref)
        hist_ref[...] = jnp.zeros_like(hist_ref)

    prv = jnp.maximum(n - 1, 0)
    jp = prv % n_col
    stream_start = ((prv // n_col) % tiles_per_stream) == 0

    ec = min(tm, PROJ_EPILOGUE_ROWS)

    def body(prev_ref, next_ref):
        prev_ref[0:pad, :] = jnp.where(stream_start, c0_ref[0], hist_ref[jp])
        last = prev_ref[tm:tm + pad, :]
        hist_ref[jp] = last
        tail_ref[0] = last
        outs = []
        for r in range(0, tm, ec):
            raw = prev_ref[r:r + pad + ec, :]
            conv = raw * cw_ref[CONV_WIDTH - 1:CONV_WIDTH, :] + cb_ref[...]
            for back in range(1, CONV_WIDTH):
                k = CONV_WIDTH - 1 - back
                conv = conv + pltpu.roll(raw, back, 0) * cw_ref[k:k + 1, :]
            outs.append(_act_silu(conv[pad:pad + ec, :]).astype(o_ref.dtype))
        res = _dot(xn_ref[...], w_ref[...])
        for i, r in enumerate(range(0, tm, ec)):
            o_ref[r:r + ec, :] = outs[i]
        next_ref[pad:pad + tm, :] = res

    _by_parity(n, body, acc0_ref, acc1_ref)


def _proj_conv(xn, w, conv_w, conv_b, c0_padded, n_streams, tm, tn):
    rows, d = xn.shape
    n_out = w.shape[1]
    n_row, n_col = rows // tm, n_out // tn
    tiles_per_stream = n_row // n_streams
    cur, prv = _tile_maps(n_row, n_col)
    stream = lambda n: prv(n) // n_col // tiles_per_stream
    return pl.pallas_call(
        functools.partial(_proj_conv_kernel, n_col=n_col, tiles_per_stream=tiles_per_stream),
        grid=(n_row * n_col + 1,),
        in_specs=[
            pl.BlockSpec((tm, d), lambda n: (cur(n) // n_col, 0)),
            pl.BlockSpec((d, tn), lambda n: (0, cur(n) % n_col)),
            pl.BlockSpec((CONV_WIDTH, tn), lambda n: (0, prv(n) % n_col)),
            pl.BlockSpec((1, tn), lambda n: (0, prv(n) % n_col)),
            pl.BlockSpec((1, CONV_PAD_ROWS, tn), lambda n: (stream(n), 0, prv(n) % n_col)),
        ],
        out_specs=[
            pl.BlockSpec((tm, tn), lambda n: (prv(n) // n_col, prv(n) % n_col)),
            pl.BlockSpec((1, CONV_PAD_ROWS, tn), lambda n: (prv(n) // n_col, 0, prv(n) % n_col)),
        ],
        out_shape=[
            jax.ShapeDtypeStruct((rows, n_out), BF16),
            jax.ShapeDtypeStruct((n_row, CONV_PAD_ROWS, n_out), F32),
        ],
        scratch_shapes=[
            pltpu.VMEM((CONV_PAD_ROWS + tm, tn), F32),
            pltpu.VMEM((CONV_PAD_ROWS + tm, tn), F32),
            pltpu.VMEM((n_col, CONV_PAD_ROWS, tn), F32),
        ],
        compiler_params=_cparams("arbitrary"),
        name="proj_conv",
    )(xn, w, conv_w, conv_b, c0_padded)


def _sgu_kernel(*refs, seg_len, emit_v):
    if emit_v:
        (uv_ref, ga_ref, lng_ref, lnb_ref, wm_ref, bias_ref, wa_ref, o_ref, v_ref,
         gated0_ref, gated1_ref) = refs
    else:
        (uv_ref, ga_ref, lng_ref, lnb_ref, wm_ref, bias_ref, wa_ref, o_ref,
         gated0_ref, gated1_ref) = refs
    n = pl.program_id(0)
    tm = uv_ref.shape[0]
    d = o_ref.shape[1]
    gd = d // SGU_GROUPS

    @pl.when(n == 0)
    def _():
        gated1_ref[...] = jnp.zeros_like(gated1_ref)

    row = lax.broadcasted_iota(jnp.int32, (SGU_CHUNK, SGU_CHUNK), 0)
    col = lax.broadcasted_iota(jnp.int32, (SGU_CHUNK, SGU_CHUNK), 1)
    mask = (col <= row) & ((row // seg_len) == (col // seg_len))

    def body(prev_ref, next_ref):
        ya = _dot(prev_ref[...], wa_ref[...])
        out = (ga_ref[...].astype(F32) * ya).astype(o_ref.dtype)
        ws = [jnp.where(mask, wm_ref[g], 0.0).astype(BF16) for g in range(SGU_GROUPS)]
        gated_rows = []
        v_rows = []
        for c in range(tm // SGU_CHUNK):
            rs = slice(c * SGU_CHUNK, (c + 1) * SGU_CHUNK)
            v = uv_ref[rs, d:].astype(F32)
            mu = jnp.mean(v, axis=-1, keepdims=True)
            vc = v - mu
            var = jnp.mean(vc * vc, axis=-1, keepdims=True)
            vn = vc * lax.rsqrt(var + EPS) * lng_ref[...] + lnb_ref[...]
            v_rows.append(vn)
            vb = vn.astype(BF16)
            cols = []
            for g in range(SGU_GROUPS):
                cs = slice(g * gd, (g + 1) * gd)
                mixed = _dot(ws[g], vb[:, cs]) + bias_ref[:, cs]
                cols.append((uv_ref[rs, cs].astype(F32) * mixed).astype(BF16))
            gated_rows.append(jnp.concatenate(cols, axis=1))
        o_ref[...] = out
        for c in range(tm // SGU_CHUNK):
            rs = slice(c * SGU_CHUNK, (c + 1) * SGU_CHUNK)
            next_ref[rs, :] = gated_rows[c]
            if emit_v:
                v_ref[rs, :] = v_rows[c]

    _by_parity(n, body, gated0_ref, gated1_ref)


def _sgu(uv, gates, ln_g, ln_b, wm, bias_full, w_a, seg_len, tm, emit_v):
    rows = uv.shape[0]
    d = w_a.shape[0]
    n_row = rows // tm
    cur, prv = _tile_maps(n_row, 1)
    const2 = lambda n: (0, 0)
    out_specs = [pl.BlockSpec((tm, d), lambda n: (prv(n), 0))]
    out_shape = [jax.ShapeDtypeStruct((rows, d), BF16)]
    if emit_v:
        out_specs.append(pl.BlockSpec((tm, d), lambda n: (cur(n), 0)))
        out_shape.append(jax.ShapeDtypeStruct((rows, d), F32))
    return pl.pallas_call(
        functools.partial(_sgu_kernel, seg_len=seg_len, emit_v=emit_v),
        grid=(n_row + 1,),
        in_specs=[
            pl.BlockSpec((tm, 2 * d), lambda n: (cur(n), 0)),
            pl.BlockSpec((tm, d), lambda n: (prv(n), 0)),
            pl.BlockSpec((1, d), const2),
            pl.BlockSpec((1, d), const2),
            pl.BlockSpec((SGU_GROUPS, SGU_CHUNK, SGU_CHUNK), lambda n: (0, 0, 0)),
            pl.BlockSpec((SGU_CHUNK, d), const2),
            pl.BlockSpec((d, d), const2),
        ],
        out_specs=out_specs,
        out_shape=out_shape,
        scratch_shapes=[pltpu.VMEM((tm, d), BF16), pltpu.VMEM((tm, d), BF16)],
        compiler_params=_cparams("arbitrary"),
        name="sgu",
    )(uv, gates, ln_g, ln_b, wm, bias_full, w_a)


def _ssd_kernel(*refs, n_tiles, valid_last, conv_in_kernel, chunks):
    if conv_in_kernel:
        (xbc_ref, dt_ref, z_ref, a_ref, d_ref, ng_ref, h0_ref, cw_ref, cb_ref, c0_ref,
         y_ref, hout_ref, cout_ref, ht_ref, cbuf_ref) = refs
    else:
        xbc_ref, dt_ref, z_ref, a_ref, d_ref, ng_ref, h0_ref, y_ref, hout_ref, ht_ref = refs
    t = pl.program_id(1)
    q = SSD_CHUNK
    d_ssm = y_ref.shape[1]
    n_heads = d_ssm // SSM_HEADDIM
    gw = d_ssm // SSM_GROUPS
    pair = 2 * SSM_HEADDIM
    n_bc = SSM_GROUPS * SSM_STATE
    assert pair == LANES and q == SSM_HEADDIM

    @pl.when(t == 0)
    def _init():
        for g in range(SSM_GROUPS):
            ht_ref[g] = h0_ref[0, g].T
        if conv_in_kernel:
            cbuf_ref[0:CONV_PAD_ROWS, :] = jnp.zeros((CONV_PAD_ROWS, cbuf_ref.shape[1]), F32)
            cbuf_ref[CONV_PAD_ROWS - (CONV_WIDTH - 1):CONV_PAD_ROWS, :] = c0_ref[0]

    tq = chunks * q
    n_pairs = gw // pair
    if conv_in_kernel:
        cbuf_ref[CONV_PAD_ROWS:CONV_PAD_ROWS + tq, :] = xbc_ref[...].astype(F32)
        conv = cb_ref[...]
        for k in range(CONV_WIDTH):
            r0 = CONV_PAD_ROWS - (CONV_WIDTH - 1) + k
            conv = conv + cbuf_ref[r0:r0 + tq, :] * cw_ref[k:k + 1, :]

        @pl.when(t == n_tiles - 1)
        def _conv_out():
            r0 = CONV_PAD_ROWS + valid_last - (CONV_WIDTH - 1)
            cout_ref[0] = cbuf_ref[r0:r0 + CONV_WIDTH - 1, :]

        cbuf_ref[0:CONV_PAD_ROWS, :] = cbuf_ref[tq:tq + CONV_PAD_ROWS, :]
        xc_all = _act_silu(conv).astype(BF16)
    else:
        xc_all = xbc_ref[...]

    row = lax.broadcasted_iota(jnp.int32, (q, gw), 0)
    pos = lax.broadcasted_iota(jnp.int32, (q, gw), 1) & (SSM_HEADDIM - 1)
    on_diag = row == pos
    causal = row >= pos
    hrow = lax.broadcasted_iota(jnp.int32, (n_heads, d_ssm), 0)
    hlane = lax.broadcasted_iota(jnp.int32, (n_heads, d_ssm), 1)
    expand = (hrow == (hlane // SSM_HEADDIM)).astype(BF16)
    r2 = lax.broadcasted_iota(jnp.int32, (q, q), 0)
    c2 = lax.broadcasted_iota(jnp.int32, (q, q), 1)
    lower = (c2 <= r2).astype(BF16)
    eye = (lax.broadcasted_iota(jnp.int32, (SSM_STATE, SSM_STATE), 0)
           == lax.broadcasted_iota(jnp.int32, (SSM_STATE, SSM_STATE), 1)).astype(BF16)
    first_head = lax.broadcasted_iota(jnp.int32, (q, pair), 1) < SSM_HEADDIM

    state = []
    for g in range(SSM_GROUPS):
        htg = ht_ref[g]
        state.append([htg[:, j * pair:(j + 1) * pair] for j in range(n_pairs)])
    cums = []
    for c in range(chunks):
        dt = dt_ref[c * q:(c + 1) * q, :]
        da_hi, da_lo = _split_bf16(dt * a_ref[...])
        cum = _split_bf16(_dot(lower, da_hi) + _dot(lower, da_lo))
        cums.append((cum, dt.astype(BF16)))

    def decay_terms(c, g):
        rs = slice(c * q, (c + 1) * q)
        gs = slice(g * gw, (g + 1) * gw)
        ns = slice(d_ssm + g * SSM_STATE, d_ssm + (g + 1) * SSM_STATE)
        (cum_hi, cum_lo), dt_bf = cums[c]
        colb = _dot(cum_hi, expand[:, gs]) + _dot(cum_lo, expand[:, gs])
        rowb = jnp.sum(jnp.where(on_diag, colb, 0.0), axis=0, keepdims=True)
        mw = jnp.where(causal, jnp.exp(jnp.minimum(colb - rowb, 0.0)), 0.0)
        expc = jnp.exp(colb)
        tailw = jnp.exp(jnp.minimum(colb[q - 1:q, :] - rowb, 0.0))
        xs = xc_all[rs, gs].astype(F32)
        xdt = (xs * _dot(dt_bf, expand[:, gs])).astype(BF16)
        bg = xc_all[rs, ns]
        cg = xc_all[rs, n_bc + ns.start:n_bc + ns.stop]
        bb = jnp.concatenate([bg, bg], axis=0)
        cb2 = _dot_nt(cg, bb)
        bt2 = _dot_nt(eye, bb)
        return mw, expc, tailw, xs, xdt, cg, cb2, bt2

    def apply_state(c, g, terms):
        mw, expc, tailw, xs, xdt, cg, cb2, bt2 = terms
        rs = slice(c * q, (c + 1) * q)
        gs = slice(g * gw, (g + 1) * gw)
        decl = expc[q - 1:q, :]
        yst = _dot(cg, jnp.concatenate(state[g], axis=1).astype(BF16))
        ys = []
        for j in range(n_pairs):
            js = slice(j * pair, (j + 1) * pair)
            m2 = (cb2 * mw[:, js]).astype(BF16)
            xd2 = xdt[:, js]
            zero = jnp.zeros_like(xd2)
            xbd = jnp.concatenate([jnp.where(first_head, xd2, zero),
                                   jnp.where(first_head, zero, xd2)], axis=0)
            ys.append(_dot(m2, xbd) + expc[:, js] * yst[:, js]
                      + d_ref[:, g * gw + j * pair:g * gw + (j + 1) * pair] * xs[:, js])
            btw = (bt2 * tailw[:, js]).astype(BF16)
            state[g][j] = state[g][j] * decl[:, js] + _dot(btw, xbd)
        yg = jnp.concatenate(ys, axis=1) * z_ref[rs, gs].astype(F32)
        return _rms(yg, ng_ref[:, gs]).astype(y_ref.dtype)

    items = [(c, g) for c in range(chunks) for g in range(SSM_GROUPS)]
    y_out = [[None] * SSM_GROUPS for _ in range(chunks)]
    ahead = [decay_terms(*item) for item in items[:SSD_TERMS_AHEAD]]
    for k, (c, g) in enumerate(items):
        if k + SSD_TERMS_AHEAD < len(items):
            ahead.append(decay_terms(*items[k + SSD_TERMS_AHEAD]))
        y_out[c][g] = apply_state(c, g, ahead.pop(0))

    for c in range(chunks):
        for g in range(SSM_GROUPS):
            y_ref[c * q:(c + 1) * q, g * gw:(g + 1) * gw] = y_out[c][g]
    for g in range(SSM_GROUPS):
        ht_ref[g] = jnp.concatenate(state[g], axis=1)

    @pl.when(t == n_tiles - 1)
    def _state_out():
        for g in range(SSM_GROUPS):
            hout_ref[0, g] = ht_ref[g].T


def _ssd(xbc, dt, zs, a_row, d_full, norm_g, h0, n_streams, n_tiles, valid_last, chunks,
         conv=None):
    rows, conv_dim = xbc.shape
    d_ssm = zs.shape[1]
    n_heads = dt.shape[1]
    gw = d_ssm // SSM_GROUPS
    q = chunks * SSD_CHUNK
    rb = lambda s, t: (s * n_tiles + t, 0)
    const2 = lambda s, t: (0, 0)
    per_stream3 = lambda s, t: (s, 0, 0)
    state_spec = pl.BlockSpec((1, SSM_GROUPS, gw, SSM_STATE), lambda s, t: (s, 0, 0, 0))
    in_specs = [
        pl.BlockSpec((q, conv_dim), rb),
        pl.BlockSpec((q, n_heads), rb),
        pl.BlockSpec((q, d_ssm), rb),
        pl.BlockSpec((1, n_heads), const2),
        pl.BlockSpec((1, d_ssm), const2),
        pl.BlockSpec((1, d_ssm), const2),
        state_spec,
    ]
    args = [xbc, dt, zs, a_row, d_full, norm_g, h0]
    out_specs = [pl.BlockSpec((q, d_ssm), rb), state_spec]
    out_shape = [
        jax.ShapeDtypeStruct((rows, d_ssm), BF16),
        jax.ShapeDtypeStruct((n_streams, SSM_GROUPS, gw, SSM_STATE), F32),
    ]
    scratch = [pltpu.VMEM((SSM_GROUPS, SSM_STATE, gw), F32)]
    if conv is not None:
        in_specs += [
            pl.BlockSpec((CONV_WIDTH, conv_dim), const2),
            pl.BlockSpec((1, conv_dim), const2),
            pl.BlockSpec((1, CONV_WIDTH - 1, conv_dim), per_stream3),
        ]
        args += list(conv)
        out_specs.append(pl.BlockSpec((1, CONV_WIDTH - 1, conv_dim), per_stream3))
        out_shape.append(jax.ShapeDtypeStruct((n_streams, CONV_WIDTH - 1, conv_dim), F32))
        scratch.append(pltpu.VMEM((CONV_PAD_ROWS + q, conv_dim), F32))
    return pl.pallas_call(
        functools.partial(_ssd_kernel, n_tiles=n_tiles, valid_last=valid_last,
                          conv_in_kernel=conv is not None, chunks=chunks),
        grid=(n_streams, n_tiles),
        in_specs=in_specs,
        out_specs=out_specs,
        out_shape=out_shape,
        scratch_shapes=scratch,
        compiler_params=_cparams("arbitrary", "arbitrary"),
        name="ssd",
    )(*args)


def _outproj_kernel(yn_ref, gaya_ref, gb_ref, x_ref, wb_ref, wo_ref, o_ref):
    yb = _dot(yn_ref[...], wb_ref[...])
    merged = gaya_ref[...].astype(F32) + gb_ref[...].astype(F32) * yb
    o_ref[...] = x_ref[...] + _dot(merged.astype(BF16), wo_ref[...])


def _outproj(yn, gaya, gates, x, w_b, w_o, tm):
    rows, d = x.shape
    d_ssm = yn.shape[1]
    return pl.pallas_call(
        _outproj_kernel,
        grid=(rows // tm,),
        in_specs=[
            pl.BlockSpec((tm, d_ssm), lambda i: (i, 0)),
            pl.BlockSpec((tm, d), lambda i: (i, 0)),
            pl.BlockSpec((tm, d), lambda i: (i, 1)),
            pl.BlockSpec((tm, d), lambda i: (i, 0)),
            pl.BlockSpec((d_ssm, d), lambda i: (0, 0), pipeline_mode=pl.Buffered(1)),
            pl.BlockSpec((d, d), lambda i: (0, 0), pipeline_mode=pl.Buffered(1)),
        ],
        out_specs=pl.BlockSpec((tm, d), lambda i: (i, 0)),
        out_shape=jax.ShapeDtypeStruct((rows, d), F32),
        compiler_params=_cparams("arbitrary"),
        name="outproj",
    )(yn, gaya, gates, x, w_b, w_o)


def _ffn_kernel(x_ref, g_ref, wg_ref, wu_ref, wd_ref, gf_ref, o_ref, hf_ref, acc_ref, *, n_ff):
    j = pl.program_id(1)

    @pl.when(j == 0)
    def _():
        hf_ref[...] = _rms(x_ref[...], g_ref[...]).astype(BF16)
        acc_ref[...] = jnp.zeros_like(acc_ref)

    hf = hf_ref[...]
    gate = _dot(hf, wg_ref[...])
    up = _dot(hf, wu_ref[...])
    mid = (gate * jax.nn.sigmoid(gate) * up).astype(BF16)
    acc_ref[...] += _dot(mid, wd_ref[...])

    @pl.when(j == n_ff - 1)
    def _():
        o_ref[...] = _rms(x_ref[...] + acc_ref[...], gf_ref[...])


def _ffn(x, g, w_gate, w_up, w_down, g_final, tm, tf):
    rows, d = x.shape
    d_ff = w_gate.shape[1]
    n_ff = d_ff // tf
    return pl.pallas_call(
        functools.partial(_ffn_kernel, n_ff=n_ff),
        grid=(rows // tm, n_ff),
        in_specs=[
            pl.BlockSpec((tm, d), lambda i, j: (i, 0)),
            pl.BlockSpec((1, d), lambda i, j: (0, 0)),
            pl.BlockSpec((d, tf), lambda i, j: (0, j)),
            pl.BlockSpec((d, tf), lambda i, j: (0, j)),
            pl.BlockSpec((tf, d), lambda i, j: (j, 0)),
            pl.BlockSpec((1, d), lambda i, j: (0, 0)),
        ],
        out_specs=pl.BlockSpec((tm, d), lambda i, j: (i, 0)),
        out_shape=jax.ShapeDtypeStruct((rows, d), F32),
        scratch_shapes=[pltpu.VMEM((tm, d), BF16), pltpu.VMEM((tm, d), F32)],
        compiler_params=_cparams("arbitrary", "arbitrary"),
        name="ffn",
    )(x, g, w_gate, w_up, w_down, g_final)


def _pick_tile(rows, preferred):
    tile = min(rows, preferred)
    assert rows % tile == 0, (rows, tile)
    return tile


def _layer(x, h0, c0, p, g_final, want_v):
    n_streams, length, d = x.shape
    rows = n_streams * length
    x2 = x.reshape(rows, d)
    d_ssm = p["w_b"].shape[0]
    n_heads = p["dt_bias"].shape[0]
    conv_dim = p["conv_w"].shape[1]
    tm = _pick_tile(rows, 1024)
    tn = 1024
    xn = _rmsnorm_cast(x2, p["norm_mix_g"].reshape(1, d), tm)
    uv = _proj(xn, p["w_uv"], None, _act_gelu, BF16, tm, tn)
    zs = _proj(xn, p["w_z"], None, _act_silu, BF16, tm, tn)
    dt = _proj(xn, p["w_dt"], p["dt_bias"].reshape(1, n_heads), _act_softplus, F32, tm, n_heads)
    gates = _proj(xn, p["w_gates"], None, _sigmoid, BF16, tm, tn)
    conv_b = p["conv_b"].reshape(1, conv_dim)
    fuse_conv = length % tm == 0
    if fuse_conv:
        c0_padded = jnp.pad(c0, ((0, 0), (CONV_PAD_ROWS - (CONV_WIDTH - 1), 0), (0, 0)))
        xbc, conv_tail = _proj_conv(xn, p["w_xbc"], p["conv_w"], conv_b, c0_padded, n_streams,
                                    tm, tn)
    else:
        xbc = _proj(xn, p["w_xbc"], None, _act_none, BF16, tm, tn)

    seg_len = min(length, SGU_CHUNK)
    assert SGU_CHUNK % seg_len == 0 and length % seg_len == 0
    rep = SGU_CHUNK // seg_len
    wm = jnp.tile(p["sgu_w"][:, :seg_len, :seg_len], (1, rep, rep))
    bias_full = jnp.repeat(jnp.tile(p["sgu_b"][:, :seg_len], (1, rep)).T, d // SGU_GROUPS, axis=1)
    sgu_out = _sgu(uv, gates, p["sgu_ln_g"].reshape(1, d), p["sgu_ln_b"].reshape(1, d),
                   wm, bias_full, p["w_a"], seg_len, _pick_tile(rows, 512), want_v)
    gaya = sgu_out[0]
    v_norm = sgu_out[1].reshape(n_streams, length, d) if want_v else None

    chunks = SSD_CHUNKS_PER_STEP if length % (SSD_CHUNKS_PER_STEP * SSD_CHUNK) == 0 else 1
    q = chunks * SSD_CHUNK
    n_tiles = -(-length // q)
    padded = n_tiles * q
    valid_last = length - (n_tiles - 1) * q

    def pad_rows(arr):
        if padded == length:
            return arr
        arr = arr.reshape(n_streams, length, arr.shape[-1])
        arr = jnp.pad(arr, ((0, 0), (0, padded - length), (0, 0)))
        return arr.reshape(n_streams * padded, arr.shape[-1])

    gw = d_ssm // SSM_GROUPS
    a_row = (-jnp.exp(p["a_log"].astype(F32))).reshape(1, n_heads)
    d_full = jnp.repeat(p["d_skip"].astype(F32), SSM_HEADDIM).reshape(1, d_ssm)
    ssd = functools.partial(_ssd, pad_rows(xbc), pad_rows(dt), pad_rows(zs), a_row, d_full,
                            p["ssm_norm_g"].reshape(1, d_ssm),
                            h0.reshape(n_streams, SSM_GROUPS, gw, SSM_STATE),
                            n_streams, n_tiles, valid_last, chunks)
    if fuse_conv:
        yn, h_new = ssd()
        tiles_per_stream = length // tm
        c_new = conv_tail[tiles_per_stream - 1::tiles_per_stream,
                          CONV_PAD_ROWS - (CONV_WIDTH - 1):, :]
    else:
        yn, h_new, c_new = ssd(conv=(p["conv_w"], conv_b, c0))
    if padded != length:
        yn = yn.reshape(n_streams, padded, d_ssm)[:, :length].reshape(rows, d_ssm)
    h_new = h_new.reshape(n_streams, n_heads, SSM_HEADDIM, SSM_STATE)

    x1 = _outproj(yn, gaya, gates, x2, p["w_b"], p["w_o"], _pick_tile(rows, 256))
    y = _ffn(x1, p["norm_ffn_g"].reshape(1, d), p["w_gate"], p["w_up"], p["w_down"],
             g_final.reshape(1, d), _pick_tile(rows, 512), 512)
    return y.reshape(n_streams, length, d), h_new, c_new, v_norm


def _layer_params(i, norm_mix_g, w_in, sgu_ln_g, sgu_ln_b, sgu_w, sgu_b, w_a, conv_w, conv_b,
                  dt_bias, a_log, d_skip, ssm_norm_g, w_b, w_o, norm_ffn_g, w_gate, w_up, w_down):
    d = w_in.shape[1]
    d_ssm = w_b.shape[1]
    conv_dim = conv_w.shape[2]
    n_heads = dt_bias.shape[1]
    s0 = 2 * d
    s1 = s0 + d_ssm
    s2 = s1 + conv_dim
    s3 = s2 + n_heads
    wi = w_in[i]
    return dict(
        norm_mix_g=norm_mix_g[i],
        w_uv=wi[:, :s0].astype(BF16), w_z=wi[:, s0:s1].astype(BF16),
        w_xbc=wi[:, s1:s2].astype(BF16), w_dt=wi[:, s2:s3].astype(BF16),
        w_gates=wi[:, s3:].astype(BF16),
        sgu_ln_g=sgu_ln_g[i], sgu_ln_b=sgu_ln_b[i], sgu_w=sgu_w[i], sgu_b=sgu_b[i],
        w_a=w_a[i].astype(BF16), conv_w=conv_w[i], conv_b=conv_b[i], dt_bias=dt_bias[i],
        a_log=a_log[i], d_skip=d_skip[i], ssm_norm_g=ssm_norm_g[i],
        w_b=w_b[i].astype(BF16), w_o=w_o[i].astype(BF16), norm_ffn_g=norm_ffn_g[i],
        w_gate=w_gate[i].astype(BF16), w_up=w_up[i].astype(BF16), w_down=w_down[i].astype(BF16),
    )


def kernel(x_prompt, x_sample, state_ssm, state_conv, norm_mix_g, w_in, sgu_ln_g, sgu_ln_b, sgu_w, sgu_b, w_a, conv_w, conv_b, dt_bias, a_log, d_skip, ssm_norm_g, w_b, w_o, norm_ffn_g, w_gate, w_up, w_down, norm_final_g):
    depth = w_in.shape[0]
    assert depth == 1, "multi-layer stacks need the final norm split from the FFN kernel"
    n_prompt = x_prompt.shape[0]
    n_heads = dt_bias.shape[1]
    conv_dim = conv_w.shape[2]
    p = _layer_params(0, norm_mix_g, w_in, sgu_ln_g, sgu_ln_b, sgu_w, sgu_b, w_a, conv_w, conv_b,
                      dt_bias, a_log, d_skip, ssm_norm_g, w_b, w_o, norm_ffn_g, w_gate, w_up,
                      w_down)
    h0_p = jnp.zeros((n_prompt, n_heads, SSM_HEADDIM, SSM_STATE), F32)
    c0_p = jnp.zeros((n_prompt, CONV_WIDTH - 1, conv_dim), F32)
    y_p, h_p, c_p, _ = _layer(x_prompt, h0_p, c0_p, p, norm_final_g, want_v=False)
    y_s, h_s, c_s, v_s = _layer(x_sample, state_ssm[0], state_conv[0], p, norm_final_g,
                                want_v=True)
    return (y_p, y_s, h_p[None], c_p[None], h_s[None], c_s[None], v_s[None])
```

```python
import functools
from typing import NamedTuple

import jax
import jax.numpy as jnp
from jax import lax
from jax.experimental import pallas as pl
from jax.experimental.pallas import tpu as pltpu

F32 = jnp.float32
BF16 = jnp.bfloat16

EPS = 1e-6
SGU_CHUNK = 128
SGU_GROUPS = 8
SSM_HEADDIM = 64
SSM_GROUPS = 8
SSM_STATE = 128
CONV_WIDTH = 4
SSD_CHUNK = 64
SSD_CHUNKS_PER_STEP = 8
SSD_TERMS_AHEAD = 1

V7X_VMEM_LIMIT_BYTES = 56 * 1024 * 1024
LANES = 128
CONV_PAD_ROWS = 8
PROJ_EPILOGUE_ROWS = 32
CONV_EPILOGUE_ROWS = 32


def _cparams(*sem):
    return pltpu.CompilerParams(dimension_semantics=sem, vmem_limit_bytes=V7X_VMEM_LIMIT_BYTES)


def _dot(a, b):
    return jnp.dot(a, b, preferred_element_type=F32)


def _dot_nt(a, b):
    return lax.dot_general(a, b, (((1,), (1,)), ((), ())), preferred_element_type=F32)


def _rms(x, g):
    return x * lax.rsqrt(jnp.mean(x * x, axis=-1, keepdims=True) + EPS) * g


def _split_bf16(x):
    hi = x.astype(BF16)
    lo = (x - hi.astype(F32)).astype(BF16)
    return hi, lo


def _sigmoid(x):
    return 0.5 * jnp.tanh(0.5 * x) + 0.5


def _act_gelu(x):
    return jax.nn.gelu(x, approximate=True)


def _act_silu(x):
    h = 0.5 * x
    return h + h * jnp.tanh(h)


def _act_softplus(x):
    return jnp.maximum(x, 0.0) + jnp.log1p(jnp.exp(-jnp.abs(x)))


def _act_none(x):
    return x


def _rmsnorm_kernel(x_ref, g_ref, o_ref):
    o_ref[...] = _rms(x_ref[...], g_ref[...]).astype(o_ref.dtype)


def _rmsnorm_cast(x, g, tm):
    rows, d = x.shape
    return pl.pallas_call(
        _rmsnorm_kernel,
        grid=(rows // tm,),
        in_specs=[pl.BlockSpec((tm, d), lambda i: (i, 0)), pl.BlockSpec((1, d), lambda i: (0, 0))],
        out_specs=pl.BlockSpec((tm, d), lambda i: (i, 0)),
        out_shape=jax.ShapeDtypeStruct((rows, d), BF16),
        compiler_params=_cparams("arbitrary"),
        name="rmsnorm",
    )(x, g)


def _by_parity(n, body, acc0_ref, acc1_ref):
    @pl.when(n % 2 == 0)
    def _():
        body(acc1_ref, acc0_ref)

    @pl.when(n % 2 == 1)
    def _():
        body(acc0_ref, acc1_ref)


def _proj_kernel(*refs, act, has_bias):
    if has_bias:
        xn_ref, w_ref, b_ref, o_ref, acc0_ref, acc1_ref = refs
    else:
        xn_ref, w_ref, o_ref, acc0_ref, acc1_ref = refs
    n = pl.program_id(0)

    @pl.when(n == 0)
    def _():
        acc1_ref[...] = jnp.zeros_like(acc1_ref)

    tm = o_ref.shape[0]
    ec = min(tm, PROJ_EPILOGUE_ROWS)

    def body(prev_ref, next_ref):
        outs = []
        for r in range(0, tm, ec):
            prev = prev_ref[r:r + ec, :]
            if has_bias:
                prev = prev + b_ref[...]
            outs.append(act(prev).astype(o_ref.dtype))
        res = _dot(xn_ref[...], w_ref[...])
        for i, r in enumerate(range(0, tm, ec)):
            o_ref[r:r + ec, :] = outs[i]
        next_ref[...] = res

    _by_parity(n, body, acc0_ref, acc1_ref)


def _tile_maps(n_row, n_col):
    last = n_row * n_col - 1
    cur = lambda n: jnp.minimum(n, last)
    prv = lambda n: jnp.maximum(n - 1, 0)
    return cur, prv


class _Cols(NamedTuple):
    w: jax.Array
    start: int
    width: int

    def first_block(self, tn):
        assert self.start % tn == 0 and self.width % tn == 0, (self.start, self.width, tn)
        return self.start // tn


def _proj(xn, cols, bias, act, out_dtype, tm, tn):
    rows, d = xn.shape
    n_out = cols.width
    col0 = cols.first_block(tn)
    n_row, n_col = rows // tm, n_out // tn
    cur, prv = _tile_maps(n_row, n_col)
    in_specs = [
        pl.BlockSpec((tm, d), lambda n: (cur(n) // n_col, 0)),
        pl.BlockSpec((d, tn), lambda n: (0, col0 + cur(n) % n_col)),
    ]
    args = [xn, cols.w]
    if bias is not None:
        in_specs.append(pl.BlockSpec((1, tn), lambda n: (0, prv(n) % n_col)))
        args.append(bias)
    return pl.pallas_call(
        functools.partial(_proj_kernel, act=act, has_bias=bias is not None),
        grid=(n_row * n_col + 1,),
        in_specs=in_specs,
        out_specs=pl.BlockSpec((tm, tn), lambda n: (prv(n) // n_col, prv(n) % n_col)),
        out_shape=jax.ShapeDtypeStruct((rows, n_out), out_dtype),
        scratch_shapes=[pltpu.VMEM((tm, tn), F32), pltpu.VMEM((tm, tn), F32)],
        compiler_params=_cparams("arbitrary"),
        name="proj",
    )(*args)


def _proj_conv_kernel(xn_ref, w_ref, cw_ref, cb_ref, c0_ref, o_ref, tail_ref, acc0_ref, acc1_ref,
                      hist_ref, *, n_col, tiles_per_stream):
    n = pl.program_id(0)
    tm = o_ref.shape[0]
    pad = CONV_PAD_ROWS

    @pl.when(n == 0)
    def _():
        acc1_ref[...] = jnp.zeros_like(acc1_ref)
        hist_ref[...] = jnp.zeros_like(hist_ref)

    prv = jnp.maximum(n - 1, 0)
    jp = prv % n_col
    stream_start = ((prv // n_col) % tiles_per_stream) == 0

    ec = min(tm, CONV_EPILOGUE_ROWS)

    def body(prev_ref, next_ref):
        prev_ref[0:pad, :] = jnp.where(stream_start, c0_ref[0], hist_ref[jp])
        last = prev_ref[tm:tm + pad, :]
        hist_ref[jp] = last
        tail_ref[0] = last
        outs = []
        for r in range(0, tm, ec):
            raw = prev_ref[r:r + pad + ec, :]
            conv = raw * cw_ref[CONV_WIDTH - 1:CONV_WIDTH, :] + cb_ref[...]
            for back in range(1, CONV_WIDTH):
                k = CONV_WIDTH - 1 - back
                conv = conv + pltpu.roll(raw, back, 0) * cw_ref[k:k + 1, :]
            outs.append(_act_silu(conv[pad:pad + ec, :]).astype(o_ref.dtype))
        res = _dot(xn_ref[...], w_ref[...])
        for i, r in enumerate(range(0, tm, ec)):
            o_ref[r:r + ec, :] = outs[i]
        next_ref[pad:pad + tm, :] = res

    _by_parity(n, body, acc0_ref, acc1_ref)


def _proj_conv(xn, cols, conv_w, conv_b, c0_padded, n_streams, tm, tn):
    rows, d = xn.shape
    n_out = cols.width
    col0 = cols.first_block(tn)
    n_row, n_col = rows // tm, n_out // tn
    tiles_per_stream = n_row // n_streams
    cur, prv = _tile_maps(n_row, n_col)
    stream = lambda n: prv(n) // n_col // tiles_per_stream
    return pl.pallas_call(
        functools.partial(_proj_conv_kernel, n_col=n_col, tiles_per_stream=tiles_per_stream),
        grid=(n_row * n_col + 1,),
        in_specs=[
            pl.BlockSpec((tm, d), lambda n: (cur(n) // n_col, 0)),
            pl.BlockSpec((d, tn), lambda n: (0, col0 + cur(n) % n_col)),
            pl.BlockSpec((CONV_WIDTH, tn), lambda n: (0, prv(n) % n_col)),
            pl.BlockSpec((1, tn), lambda n: (0, prv(n) % n_col)),
            pl.BlockSpec((1, CONV_PAD_ROWS, tn), lambda n: (stream(n), 0, prv(n) % n_col)),
        ],
        out_specs=[
            pl.BlockSpec((tm, tn), lambda n: (prv(n) // n_col, prv(n) % n_col)),
            pl.BlockSpec((1, CONV_PAD_ROWS, tn), lambda n: (prv(n) // n_col, 0, prv(n) % n_col)),
        ],
        out_shape=[
            jax.ShapeDtypeStruct((rows, n_out), BF16),
            jax.ShapeDtypeStruct((n_row, CONV_PAD_ROWS, n_out), F32),
        ],
        scratch_shapes=[
            pltpu.VMEM((CONV_PAD_ROWS + tm, tn), F32),
            pltpu.VMEM((CONV_PAD_ROWS + tm, tn), F32),
            pltpu.VMEM((n_col, CONV_PAD_ROWS, tn), F32),
        ],
        compiler_params=_cparams("arbitrary"),
        name="proj_conv",
    )(xn, cols.w, conv_w, conv_b, c0_padded)


def _sgu_kernel(*refs, seg_len, emit_v):
    if emit_v:
        (uv_ref, ga_ref, lng_ref, lnb_ref, wm_ref, bias_ref, wa_ref, o_ref, v_ref,
         gated0_ref, gated1_ref) = refs
    else:
        (uv_ref, ga_ref, lng_ref, lnb_ref, wm_ref, bias_ref, wa_ref, o_ref,
         gated0_ref, gated1_ref) = refs
    n = pl.program_id(0)
    tm = uv_ref.shape[0]
    d = o_ref.shape[1]
    gd = d // SGU_GROUPS

    @pl.when(n == 0)
    def _():
        gated1_ref[...] = jnp.zeros_like(gated1_ref)

    row = lax.broadcasted_iota(jnp.int32, (SGU_CHUNK, SGU_CHUNK), 0)
    col = lax.broadcasted_iota(jnp.int32, (SGU_CHUNK, SGU_CHUNK), 1)
    mask = (col <= row) & ((row // seg_len) == (col // seg_len))

    n_chunks = tm // SGU_CHUNK
    pw = d // n_chunks

    def body(prev_ref, next_ref):
        ws = [jnp.where(mask, wm_ref[g], 0.0).astype(BF16) for g in range(SGU_GROUPS)]
        gated_rows = []
        v_rows = []
        outs = []
        for c in range(n_chunks):
            ps = slice(c * pw, (c + 1) * pw)
            ya = _dot(prev_ref[...], wa_ref[:, ps])
            outs.append((ga_ref[:, ps].astype(F32) * ya).astype(o_ref.dtype))
            rs = slice(c * SGU_CHUNK, (c + 1) * SGU_CHUNK)
            v = uv_ref[rs, d:].astype(F32)
            mu = jnp.mean(v, axis=-1, keepdims=True)
            vc = v - mu
            var = jnp.mean(vc * vc, axis=-1, keepdims=True)
            vn = vc * lax.rsqrt(var + EPS) * lng_ref[...] + lnb_ref[...]
            v_rows.append(vn)
            vb = vn.astype(BF16)
            cols = []
            for g in range(SGU_GROUPS):
                cs = slice(g * gd, (g + 1) * gd)
                mixed = _dot(ws[g], vb[:, cs]) + bias_ref[:, cs]
                cols.append((uv_ref[rs, cs].astype(F32) * mixed).astype(BF16))
            gated_rows.append(jnp.concatenate(cols, axis=1))
        o_ref[...] = jnp.concatenate(outs, axis=1)
        for c in range(n_chunks):
            rs = slice(c * SGU_CHUNK, (c + 1) * SGU_CHUNK)
            next_ref[rs, :] = gated_rows[c]
            if emit_v:
                v_ref[rs, :] = v_rows[c]

    _by_parity(n, body, gated0_ref, gated1_ref)


def _sgu(uv, gates, ln_g, ln_b, wm, bias_full, w_a, seg_len, tm, emit_v):
    rows = uv.shape[0]
    d = w_a.shape[0]
    n_row = rows // tm
    cur, prv = _tile_maps(n_row, 1)
    const2 = lambda n: (0, 0)
    out_specs = [pl.BlockSpec((tm, d), lambda n: (prv(n), 0))]
    out_shape = [jax.ShapeDtypeStruct((rows, d), BF16)]
    if emit_v:
        out_specs.append(pl.BlockSpec((tm, d), lambda n: (cur(n), 0)))
        out_shape.append(jax.ShapeDtypeStruct((rows, d), F32))
    return pl.pallas_call(
        functools.partial(_sgu_kernel, seg_len=seg_len, emit_v=emit_v),
        grid=(n_row + 1,),
        in_specs=[
            pl.BlockSpec((tm, 2 * d), lambda n: (cur(n), 0)),
            pl.BlockSpec((tm, d), lambda n: (prv(n), 0)),
            pl.BlockSpec((1, d), const2),
            pl.BlockSpec((1, d), const2),
            pl.BlockSpec((SGU_GROUPS, SGU_CHUNK, SGU_CHUNK), lambda n: (0, 0, 0)),
            pl.BlockSpec((SGU_CHUNK, d), const2),
            pl.BlockSpec((d, d), const2),
        ],
        out_specs=out_specs,
        out_shape=out_shape,
        scratch_shapes=[pltpu.VMEM((tm, d), BF16), pltpu.VMEM((tm, d), BF16)],
        compiler_params=_cparams("arbitrary"),
        name="sgu",
    )(uv, gates, ln_g, ln_b, wm, bias_full, w_a)


def _ssd_kernel(*refs, n_tiles, valid_last, conv_in_kernel, chunks):
    if conv_in_kernel:
        (xbc_ref, dt_ref, z_ref, a_ref, d_ref, ng_ref, h0_ref, cw_ref, cb_ref, c0_ref,
         y_ref, hout_ref, cout_ref, ht_ref, cbuf_ref) = refs
    else:
        xbc_ref, dt_ref, z_ref, a_ref, d_ref, ng_ref, h0_ref, y_ref, hout_ref, ht_ref = refs
    t = pl.program_id(1)
    q = SSD_CHUNK
    d_ssm = y_ref.shape[1]
    n_heads = d_ssm // SSM_HEADDIM
    gw = d_ssm // SSM_GROUPS
    pair = 2 * SSM_HEADDIM
    n_bc = SSM_GROUPS * SSM_STATE
    assert pair == LANES and q == SSM_HEADDIM

    @pl.when(t == 0)
    def _init():
        for g in range(SSM_GROUPS):
            ht_ref[g] = h0_ref[0, g].T
        if conv_in_kernel:
            cbuf_ref[0:CONV_PAD_ROWS, :] = jnp.zeros((CONV_PAD_ROWS, cbuf_ref.shape[1]), F32)
            cbuf_ref[CONV_PAD_ROWS - (CONV_WIDTH - 1):CONV_PAD_ROWS, :] = c0_ref[0]

    tq = chunks * q
    n_pairs = gw // pair
    if conv_in_kernel:
        cbuf_ref[CONV_PAD_ROWS:CONV_PAD_ROWS + tq, :] = xbc_ref[...].astype(F32)
        conv = cb_ref[...]
        for k in range(CONV_WIDTH):
            r0 = CONV_PAD_ROWS - (CONV_WIDTH - 1) + k
            conv = conv + cbuf_ref[r0:r0 + tq, :] * cw_ref[k:k + 1, :]

        @pl.when(t == n_tiles - 1)
        def _conv_out():
            r0 = CONV_PAD_ROWS + valid_last - (CONV_WIDTH - 1)
            cout_ref[0] = cbuf_ref[r0:r0 + CONV_WIDTH - 1, :]

        cbuf_ref[0:CONV_PAD_ROWS, :] = cbuf_ref[tq:tq + CONV_PAD_ROWS, :]
        xc_all = _act_silu(conv).astype(BF16)
    else:
        xc_all = xbc_ref[...]

    row = lax.broadcasted_iota(jnp.int32, (q, gw), 0)
    pos = lax.broadcasted_iota(jnp.int32, (q, gw), 1) & (SSM_HEADDIM - 1)
    on_diag = row == pos
    causal = row >= pos
    hrow = lax.broadcasted_iota(jnp.int32, (n_heads, d_ssm), 0)
    hlane = lax.broadcasted_iota(jnp.int32, (n_heads, d_ssm), 1)
    expand = (hrow == (hlane // SSM_HEADDIM)).astype(BF16)
    r2 = lax.broadcasted_iota(jnp.int32, (q, q), 0)
    c2 = lax.broadcasted_iota(jnp.int32, (q, q), 1)
    lower = (c2 <= r2).astype(BF16)
    eye = (lax.broadcasted_iota(jnp.int32, (SSM_STATE, SSM_STATE), 0)
           == lax.broadcasted_iota(jnp.int32, (SSM_STATE, SSM_STATE), 1)).astype(BF16)
    first_head = lax.broadcasted_iota(jnp.int32, (q, pair), 1) < SSM_HEADDIM

    state = []
    for g in range(SSM_GROUPS):
        htg = ht_ref[g]
        state.append([htg[:, j * pair:(j + 1) * pair] for j in range(n_pairs)])
    cums = []
    for c in range(chunks):
        dt = dt_ref[c * q:(c + 1) * q, :]
        da_hi, da_lo = _split_bf16(dt * a_ref[...])
        cum = _split_bf16(_dot(lower, da_hi) + _dot(lower, da_lo))
        cums.append((cum, dt.astype(BF16)))

    def decay_terms(c, g):
        rs = slice(c * q, (c + 1) * q)
        gs = slice(g * gw, (g + 1) * gw)
        ns = slice(d_ssm + g * SSM_STATE, d_ssm + (g + 1) * SSM_STATE)
        (cum_hi, cum_lo), dt_bf = cums[c]
        colb = _dot(cum_hi, expand[:, gs]) + _dot(cum_lo, expand[:, gs])
        rowb = jnp.sum(jnp.where(on_diag, colb, 0.0), axis=0, keepdims=True)
        mw = jnp.where(causal, jnp.exp(jnp.minimum(colb - rowb, 0.0)), 0.0)
        expc = jnp.exp(colb)
        tailw = jnp.exp(jnp.minimum(colb[q - 1:q, :] - rowb, 0.0))
        xs = xc_all[rs, gs].astype(F32)
        xdt = (xs * _dot(dt_bf, expand[:, gs])).astype(BF16)
        bg = xc_all[rs, ns]
        cg = xc_all[rs, n_bc + ns.start:n_bc + ns.stop]
        bb = jnp.concatenate([bg, bg], axis=0)
        cb2 = _dot_nt(cg, bb)
        bt2 = _dot_nt(eye, bb)
        return mw, expc, tailw, xs, xdt, cg, cb2, bt2

    def apply_state(c, g, terms):
        mw, expc, tailw, xs, xdt, cg, cb2, bt2 = terms
        rs = slice(c * q, (c + 1) * q)
        gs = slice(g * gw, (g + 1) * gw)
        decl = expc[q - 1:q, :]
        yst = _dot(cg, jnp.concatenate(state[g], axis=1).astype(BF16))
        ys = []
        for j in range(n_pairs):
            js = slice(j * pair, (j + 1) * pair)
            m2 = (cb2 * mw[:, js]).astype(BF16)
            xd2 = xdt[:, js]
            zero = jnp.zeros_like(xd2)
            xbd = jnp.concatenate([jnp.where(first_head, xd2, zero),
                                   jnp.where(first_head, zero, xd2)], axis=0)
            ys.append(_dot(m2, xbd) + expc[:, js] * yst[:, js]
                      + d_ref[:, g * gw + j * pair:g * gw + (j + 1) * pair] * xs[:, js])
            btw = (bt2 * tailw[:, js]).astype(BF16)
            state[g][j] = state[g][j] * decl[:, js] + _dot(btw, xbd)
        yg = jnp.concatenate(ys, axis=1) * z_ref[rs, gs].astype(F32)
        return _rms(yg, ng_ref[:, gs]).astype(y_ref.dtype)

    items = [(c, g) for c in range(chunks) for g in range(SSM_GROUPS)]
    y_out = [[None] * SSM_GROUPS for _ in range(chunks)]
    ahead = [decay_terms(*item) for item in items[:SSD_TERMS_AHEAD]]
    for k, (c, g) in enumerate(items):
        if k + SSD_TERMS_AHEAD < len(items):
            ahead.append(decay_terms(*items[k + SSD_TERMS_AHEAD]))
        y_out[c][g] = apply_state(c, g, ahead.pop(0))

    for c in range(chunks):
        for g in range(SSM_GROUPS):
            y_ref[c * q:(c + 1) * q, g * gw:(g + 1) * gw] = y_out[c][g]
    for g in range(SSM_GROUPS):
        ht_ref[g] = jnp.concatenate(state[g], axis=1)

    @pl.when(t == n_tiles - 1)
    def _state_out():
        for g in range(SSM_GROUPS):
            hout_ref[0, g] = ht_ref[g].T


def _ssd(xbc, dt, zs, a_row, d_full, norm_g, h0, n_streams, n_tiles, valid_last, chunks,
         conv=None):
    rows, conv_dim = xbc.shape
    d_ssm = zs.shape[1]
    n_heads = dt.shape[1]
    gw = d_ssm // SSM_GROUPS
    q = chunks * SSD_CHUNK
    rb = lambda s, t: (s * n_tiles + t, 0)
    const2 = lambda s, t: (0, 0)
    per_stream3 = lambda s, t: (s, 0, 0)
    state_spec = pl.BlockSpec((1, SSM_GROUPS, gw, SSM_STATE), lambda s, t: (s, 0, 0, 0))
    in_specs = [
        pl.BlockSpec((q, conv_dim), rb),
        pl.BlockSpec((q, n_heads), rb),
        pl.BlockSpec((q, d_ssm), rb),
        pl.BlockSpec((1, n_heads), const2),
        pl.BlockSpec((1, d_ssm), const2),
        pl.BlockSpec((1, d_ssm), const2),
        state_spec,
    ]
    args = [xbc, dt, zs, a_row, d_full, norm_g, h0]
    out_specs = [pl.BlockSpec((q, d_ssm), rb), state_spec]
    out_shape = [
        jax.ShapeDtypeStruct((rows, d_ssm), BF16),
        jax.ShapeDtypeStruct((n_streams, SSM_GROUPS, gw, SSM_STATE), F32),
    ]
    scratch = [pltpu.VMEM((SSM_GROUPS, SSM_STATE, gw), F32)]
    if conv is not None:
        in_specs += [
            pl.BlockSpec((CONV_WIDTH, conv_dim), const2),
            pl.BlockSpec((1, conv_dim), const2),
            pl.BlockSpec((1, CONV_WIDTH - 1, conv_dim), per_stream3),
        ]
        args += list(conv)
        out_specs.append(pl.BlockSpec((1, CONV_WIDTH - 1, conv_dim), per_stream3))
        out_shape.append(jax.ShapeDtypeStruct((n_streams, CONV_WIDTH - 1, conv_dim), F32))
        scratch.append(pltpu.VMEM((CONV_PAD_ROWS + q, conv_dim), F32))
    return pl.pallas_call(
        functools.partial(_ssd_kernel, n_tiles=n_tiles, valid_last=valid_last,
                          conv_in_kernel=conv is not None, chunks=chunks),
        grid=(n_streams, n_tiles),
        in_specs=in_specs,
        out_specs=out_specs,
        out_shape=out_shape,
        scratch_shapes=scratch,
        compiler_params=_cparams("arbitrary", "arbitrary"),
        name="ssd",
    )(*args)


def _outproj_kernel(yn_ref, gaya_ref, gb_ref, x_ref, wb_ref, wo_ref, o_ref):
    yb = _dot(yn_ref[...], wb_ref[...])
    merged = gaya_ref[...].astype(F32) + gb_ref[...].astype(F32) * yb
    o_ref[...] = x_ref[...] + _dot(merged.astype(BF16), wo_ref[...])


def _outproj(yn, gaya, gates, x, w_b, w_o, tm):
    rows, d = x.shape
    d_ssm = yn.shape[1]
    return pl.pallas_call(
        _outproj_kernel,
        grid=(rows // tm,),
        in_specs=[
            pl.BlockSpec((tm, d_ssm), lambda i: (i, 0)),
            pl.BlockSpec((tm, d), lambda i: (i, 0)),
            pl.BlockSpec((tm, d), lambda i: (i, 1)),
            pl.BlockSpec((tm, d), lambda i: (i, 0)),
            pl.BlockSpec((d_ssm, d), lambda i: (0, 0), pipeline_mode=pl.Buffered(1)),
            pl.BlockSpec((d, d), lambda i: (0, 0), pipeline_mode=pl.Buffered(1)),
        ],
        out_specs=pl.BlockSpec((tm, d), lambda i: (i, 0)),
        out_shape=jax.ShapeDtypeStruct((rows, d), F32),
        compiler_params=_cparams("arbitrary"),
        name="outproj",
    )(yn, gaya, gates, x, w_b, w_o)


def _ffn_kernel(x_ref, g_ref, wg_ref, wu_ref, wd_ref, gf_ref, o_ref, hf_ref, acc_ref, *, n_ff):
    j = pl.program_id(1)

    @pl.when(j == 0)
    def _():
        hf_ref[...] = _rms(x_ref[...], g_ref[...]).astype(BF16)
        acc_ref[...] = jnp.zeros_like(acc_ref)

    hf = hf_ref[...]
    gate = _dot(hf, wg_ref[...])
    up = _dot(hf, wu_ref[...])
    mid = (_act_silu(gate) * up).astype(BF16)
    acc_ref[...] += _dot(mid, wd_ref[...])

    @pl.when(j == n_ff - 1)
    def _():
        o_ref[...] = _rms(x_ref[...] + acc_ref[...], gf_ref[...])


def _ffn(x, g, w_gate, w_up, w_down, g_final, tm, tf):
    rows, d = x.shape
    d_ff = w_gate.shape[1]
    n_ff = d_ff // tf
    return pl.pallas_call(
        functools.partial(_ffn_kernel, n_ff=n_ff),
        grid=(rows // tm, n_ff),
        in_specs=[
            pl.BlockSpec((tm, d), lambda i, j: (i, 0)),
            pl.BlockSpec((1, d), lambda i, j: (0, 0)),
            pl.BlockSpec((d, tf), lambda i, j: (0, j)),
            pl.BlockSpec((d, tf), lambda i, j: (0, j)),
            pl.BlockSpec((tf, d), lambda i, j: (j, 0)),
            pl.BlockSpec((1, d), lambda i, j: (0, 0)),
        ],
        out_specs=pl.BlockSpec((tm, d), lambda i, j: (i, 0)),
        out_shape=jax.ShapeDtypeStruct((rows, d), F32),
        scratch_shapes=[pltpu.VMEM((tm, d), BF16), pltpu.VMEM((tm, d), F32)],
        compiler_params=_cparams("arbitrary", "arbitrary"),
        name="ffn",
    )(x, g, w_gate, w_up, w_down, g_final)


def _pick_tile(rows, preferred):
    tile = min(rows, preferred)
    assert rows % tile == 0, (rows, tile)
    return tile


def _layer(x, h0, c0, p, g_final, want_v):
    n_streams, length, d = x.shape
    rows = n_streams * length
    x2 = x.reshape(rows, d)
    d_ssm = p["w_b"].shape[0]
    n_heads = p["dt_bias"].shape[0]
    conv_dim = p["conv_w"].shape[1]
    tm = _pick_tile(rows, 1024)
    tn = 1024
    xn = _rmsnorm_cast(x2, p["norm_mix_g"].reshape(1, d), tm)
    uv = _proj(xn, p["w_uv"], None, _act_gelu, BF16, tm, tn)
    zs = _proj(xn, p["w_z"], None, _act_silu, BF16, tm, tn)
    dt = _proj(xn, p["w_dt"], p["dt_bias"].reshape(1, n_heads), _act_softplus, F32, tm, n_heads)
    gates = _proj(xn, p["w_gates"], None, _sigmoid, BF16, tm, tn)
    conv_b = p["conv_b"].reshape(1, conv_dim)
    fuse_conv = length % tm == 0
    if fuse_conv:
        c0_padded = jnp.pad(c0, ((0, 0), (CONV_PAD_ROWS - (CONV_WIDTH - 1), 0), (0, 0)))
        xbc, conv_tail = _proj_conv(xn, p["w_xbc"], p["conv_w"], conv_b, c0_padded, n_streams,
                                    tm, tn)
    else:
        xbc = _proj(xn, p["w_xbc"], None, _act_none, BF16, tm, tn)

    seg_len = min(length, SGU_CHUNK)
    assert SGU_CHUNK % seg_len == 0 and length % seg_len == 0
    rep = SGU_CHUNK // seg_len
    wm = jnp.tile(p["sgu_w"][:, :seg_len, :seg_len], (1, rep, rep))
    bias_full = jnp.repeat(jnp.tile(p["sgu_b"][:, :seg_len], (1, rep)).T, d // SGU_GROUPS, axis=1)
    sgu_out = _sgu(uv, gates, p["sgu_ln_g"].reshape(1, d), p["sgu_ln_b"].reshape(1, d),
                   wm, bias_full, p["w_a"], seg_len, _pick_tile(rows, 512), want_v)
    gaya = sgu_out[0]
    v_norm = sgu_out[1].reshape(n_streams, length, d) if want_v else None

    chunks = SSD_CHUNKS_PER_STEP if length % (SSD_CHUNKS_PER_STEP * SSD_CHUNK) == 0 else 1
    q = chunks * SSD_CHUNK
    n_tiles = -(-length // q)
    padded = n_tiles * q
    valid_last = length - (n_tiles - 1) * q

    def pad_rows(arr):
        if padded == length:
            return arr
        arr = arr.reshape(n_streams, length, arr.shape[-1])
        arr = jnp.pad(arr, ((0, 0), (0, padded - length), (0, 0)))
        return arr.reshape(n_streams * padded, arr.shape[-1])

    gw = d_ssm // SSM_GROUPS
    a_row = (-jnp.exp(p["a_log"].astype(F32))).reshape(1, n_heads)
    d_full = jnp.repeat(p["d_skip"].astype(F32), SSM_HEADDIM).reshape(1, d_ssm)
    ssd = functools.partial(_ssd, pad_rows(xbc), pad_rows(dt), pad_rows(zs), a_row, d_full,
                            p["ssm_norm_g"].reshape(1, d_ssm),
                            h0.reshape(n_streams, SSM_GROUPS, gw, SSM_STATE),
                            n_streams, n_tiles, valid_last, chunks)
    if fuse_conv:
        yn, h_new = ssd()
        tiles_per_stream = length // tm
        c_new = conv_tail[tiles_per_stream - 1::tiles_per_stream,
                          CONV_PAD_ROWS - (CONV_WIDTH - 1):, :]
    else:
        yn, h_new, c_new = ssd(conv=(p["conv_w"], conv_b, c0))
    if padded != length:
        yn = yn.reshape(n_streams, padded, d_ssm)[:, :length].reshape(rows, d_ssm)
    h_new = h_new.reshape(n_streams, n_heads, SSM_HEADDIM, SSM_STATE)

    x1 = _outproj(yn, gaya, gates, x2, p["w_b"], p["w_o"], _pick_tile(rows, 256))
    y = _ffn(x1, p["norm_ffn_g"].reshape(1, d), p["w_gate"], p["w_up"], p["w_down"],
             g_final.reshape(1, d), _pick_tile(rows, 512), 512)
    return y.reshape(n_streams, length, d), h_new, c_new, v_norm


def _layer_params(i, norm_mix_g, w_in, sgu_ln_g, sgu_ln_b, sgu_w, sgu_b, w_a, conv_w, conv_b,
                  dt_bias, a_log, d_skip, ssm_norm_g, w_b, w_o, norm_ffn_g, w_gate, w_up, w_down):
    d = w_in.shape[1]
    d_ssm = w_b.shape[1]
    conv_dim = conv_w.shape[2]
    n_heads = dt_bias.shape[1]
    s0 = 2 * d
    s1 = s0 + d_ssm
    s2 = s1 + conv_dim
    s3 = s2 + n_heads
    wi = w_in[i].astype(BF16)
    return dict(
        norm_mix_g=norm_mix_g[i],
        w_uv=_Cols(wi, 0, s0), w_z=_Cols(wi, s0, d_ssm), w_xbc=_Cols(wi, s1, conv_dim),
        w_dt=_Cols(wi[:, s2:s3], 0, n_heads), w_gates=_Cols(wi[:, s3:], 0, 2 * d),
        sgu_ln_g=sgu_ln_g[i], sgu_ln_b=sgu_ln_b[i], sgu_w=sgu_w[i], sgu_b=sgu_b[i],
        w_a=w_a[i].astype(BF16), conv_w=conv_w[i], conv_b=conv_b[i], dt_bias=dt_bias[i],
        a_log=a_log[i], d_skip=d_skip[i], ssm_norm_g=ssm_norm_g[i],
        w_b=w_b[i].astype(BF16), w_o=w_o[i].astype(BF16), norm_ffn_g=norm_ffn_g[i],
        w_gate=w_gate[i].astype(BF16), w_up=w_up[i].astype(BF16), w_down=w_down[i].astype(BF16),
    )


def kernel(x_prompt, x_sample, state_ssm, state_conv, norm_mix_g, w_in, sgu_ln_g, sgu_ln_b, sgu_w, sgu_b, w_a, conv_w, conv_b, dt_bias, a_log, d_skip, ssm_norm_g, w_b, w_o, norm_ffn_g, w_gate, w_up, w_down, norm_final_g):
    depth = w_in.shape[0]
    assert depth == 1, "multi-layer stacks need the final norm split from the FFN kernel"
    n_prompt = x_prompt.shape[0]
    n_heads = dt_bias.shape[1]
    conv_dim = conv_w.shape[2]
    p = _layer_params(0, norm_mix_g, w_in, sgu_ln_g, sgu_ln_b, sgu_w, sgu_b, w_a, conv_w, conv_b,
                      dt_bias, a_log, d_skip, ssm_norm_g, w_b, w_o, norm_ffn_g, w_gate, w_up,
                      w_down)
    h0_p = jnp.zeros((n_prompt, n_heads, SSM_HEADDIM, SSM_STATE), F32)
    c0_p = jnp.zeros((n_prompt, CONV_WIDTH - 1, conv_dim), F32)
    y_p, h_p, c_p, _ = _layer(x_prompt, h0_p, c0_p, p, norm_final_g, want_v=False)
    y_s, h_s, c_s, v_s = _layer(x_sample, state_ssm[0], state_conv[0], p, norm_final_g,
                                want_v=True)
    return (y_p, y_s, h_p[None], c_p[None], h_s[None], c_s[None], v_s[None])
```

```python
import functools
from typing import NamedTuple

import jax
import jax.numpy as jnp
from jax import lax
from jax.experimental import pallas as pl
from jax.experimental.pallas import tpu as pltpu

F32 = jnp.float32
BF16 = jnp.bfloat16

EPS = 1e-6
SGU_CHUNK = 128
SGU_GROUPS = 8
SSM_HEADDIM = 64
SSM_GROUPS = 8
SSM_STATE = 128
CONV_WIDTH = 4
SSD_CHUNK = 64
SSD_CHUNKS_PER_STEP = 8
SSD_TERMS_AHEAD = 1

V7X_VMEM_LIMIT_BYTES = 56 * 1024 * 1024
LANES = 128
CONV_PAD_ROWS = 8
PROJ_EPILOGUE_ROWS = 32
CONV_EPILOGUE_ROWS = 32


def _cparams(*sem):
    return pltpu.CompilerParams(dimension_semantics=sem, vmem_limit_bytes=V7X_VMEM_LIMIT_BYTES)


def _dot(a, b):
    return jnp.dot(a, b, preferred_element_type=F32)


def _dot_nt(a, b):
    return lax.dot_general(a, b, (((1,), (1,)), ((), ())), preferred_element_type=F32)


def _rms(x, g):
    return x * lax.rsqrt(jnp.mean(x * x, axis=-1, keepdims=True) + EPS) * g


def _split_bf16(x):
    hi = x.astype(BF16)
    lo = (x - hi.astype(F32)).astype(BF16)
    return hi, lo


def _sigmoid(x):
    return 0.5 * jnp.tanh(0.5 * x) + 0.5


def _act_gelu(x):
    return jax.nn.gelu(x, approximate=True)


def _act_silu(x):
    h = 0.5 * x
    return h + h * jnp.tanh(h)


def _act_softplus(x):
    return jnp.maximum(x, 0.0) + jnp.log1p(jnp.exp(-jnp.abs(x)))


def _act_none(x):
    return x


def _by_parity(n, body, acc0_ref, acc1_ref):
    @pl.when(n % 2 == 0)
    def _():
        body(acc1_ref, acc0_ref)

    @pl.when(n % 2 == 1)
    def _():
        body(acc0_ref, acc1_ref)


def _proj_kernel(*refs, act, has_bias, norm_tiles):
    refs = list(refs)
    if norm_tiles is not None:
        x_ref, g_ref = refs[:2]
        xn_out_ref, acc0_ref, acc1_ref, xn_ref = refs[-4:]
        mid = refs[2:-4]
    else:
        xn_ref = refs[0]
        acc0_ref, acc1_ref = refs[-2:]
        mid = refs[1:-2]
    if has_bias:
        w_ref, b_ref, o_ref = mid
    else:
        w_ref, o_ref = mid
    n = pl.program_id(0)

    @pl.when(n == 0)
    def _():
        acc1_ref[...] = jnp.zeros_like(acc1_ref)

    if norm_tiles is not None:
        n_col, n_tiles = norm_tiles

        @pl.when(jnp.minimum(n, n_tiles - 1) % n_col == 0)
        def _():
            xn = _rms(x_ref[...], g_ref[...]).astype(BF16)
            xn_ref[...] = xn
            xn_out_ref[...] = xn

    tm = o_ref.shape[0]
    ec = min(tm, PROJ_EPILOGUE_ROWS)

    def body(prev_ref, next_ref):
        outs = []
        for r in range(0, tm, ec):
            prev = prev_ref[r:r + ec, :]
            if has_bias:
                prev = prev + b_ref[...]
            outs.append(act(prev).astype(o_ref.dtype))
        res = _dot(xn_ref[...], w_ref[...])
        for i, r in enumerate(range(0, tm, ec)):
            o_ref[r:r + ec, :] = outs[i]
        next_ref[...] = res

    _by_parity(n, body, acc0_ref, acc1_ref)


def _tile_maps(n_row, n_col):
    last = n_row * n_col - 1
    cur = lambda n: jnp.minimum(n, last)
    prv = lambda n: jnp.maximum(n - 1, 0)
    return cur, prv


class _Cols(NamedTuple):
    w: jax.Array
    start: int
    width: int

    def first_block(self, tn):
        assert self.start % tn == 0 and self.width % tn == 0, (self.start, self.width, tn)
        return self.start // tn


def _proj(xn, cols, bias, act, out_dtype, tm, tn, norm_gain=None):
    rows, d = xn.shape
    n_out = cols.width
    col0 = cols.first_block(tn)
    n_row, n_col = rows // tm, n_out // tn
    cur, prv = _tile_maps(n_row, n_col)
    row_spec = pl.BlockSpec((tm, d), lambda n: (cur(n) // n_col, 0))
    in_specs = [row_spec]
    args = [xn]
    if norm_gain is not None:
        in_specs.append(pl.BlockSpec((1, d), lambda n: (0, 0)))
        args.append(norm_gain)
    in_specs.append(pl.BlockSpec((d, tn), lambda n: (0, col0 + cur(n) % n_col)))
    args.append(cols.w)
    if bias is not None:
        in_specs.append(pl.BlockSpec((1, tn), lambda n: (0, prv(n) % n_col)))
        args.append(bias)
    out_specs = [pl.BlockSpec((tm, tn), lambda n: (prv(n) // n_col, prv(n) % n_col))]
    out_shape = [jax.ShapeDtypeStruct((rows, n_out), out_dtype)]
    scratch = [pltpu.VMEM((tm, tn), F32), pltpu.VMEM((tm, tn), F32)]
    if norm_gain is not None:
        out_specs.append(row_spec)
        out_shape.append(jax.ShapeDtypeStruct((rows, d), BF16))
        scratch.append(pltpu.VMEM((tm, d), BF16))
    res = pl.pallas_call(
        functools.partial(_proj_kernel, act=act, has_bias=bias is not None,
                          norm_tiles=None if norm_gain is None else (n_col, n_row * n_col)),
        grid=(n_row * n_col + 1,),
        in_specs=in_specs,
        out_specs=out_specs,
        out_shape=out_shape,
        scratch_shapes=scratch,
        compiler_params=_cparams("arbitrary"),
        name="proj",
    )(*args)
    return res if norm_gain is not None else res[0]


def _proj_conv_kernel(xn_ref, w_ref, cw_ref, cb_ref, c0_ref, o_ref, tail_ref, acc0_ref, acc1_ref,
                      hist_ref, *, n_col, tiles_per_stream):
    n = pl.program_id(0)
    tm = o_ref.shape[0]
    pad = CONV_PAD_ROWS

    @pl.when(n == 0)
    def _():
        acc1_ref[...] = jnp.zeros_like(acc1_ref)
        hist_ref[...] = jnp.zeros_like(hist_ref)

    prv = jnp.maximum(n - 1, 0)
    jp = prv % n_col
    stream_start = ((prv // n_col) % tiles_per_stream) == 0

    ec = min(tm, CONV_EPILOGUE_ROWS)

    def body(prev_ref, next_ref):
        prev_ref[0:pad, :] = jnp.where(stream_start, c0_ref[0], hist_ref[jp])
        last = prev_ref[tm:tm + pad, :]
        hist_ref[jp] = last
        tail_ref[0] = last
        outs = []
        for r in range(0, tm, ec):
            raw = prev_ref[r:r + pad + ec, :]
            conv = raw * cw_ref[CONV_WIDTH - 1:CONV_WIDTH, :] + cb_ref[...]
            for back in range(1, CONV_WIDTH):
                k = CONV_WIDTH - 1 - back
                conv = conv + pltpu.roll(raw, back, 0) * cw_ref[k:k + 1, :]
            outs.append(_act_silu(conv[pad:pad + ec, :]).astype(o_ref.dtype))
        res = _dot(xn_ref[...], w_ref[...])
        for i, r in enumerate(range(0, tm, ec)):
            o_ref[r:r + ec, :] = outs[i]
        next_ref[pad:pad + tm, :] = res

    _by_parity(n, body, acc0_ref, acc1_ref)


def _proj_conv(xn, cols, conv_w, conv_b, c0_padded, n_streams, tm, tn):
    rows, d = xn.shape
    n_out = cols.width
    col0 = cols.first_block(tn)
    n_row, n_col = rows // tm, n_out // tn
    tiles_per_stream = n_row // n_streams
    cur, prv = _tile_maps(n_row, n_col)
    stream = lambda n: prv(n) // n_col // tiles_per_stream
    return pl.pallas_call(
        functools.partial(_proj_conv_kernel, n_col=n_col, tiles_per_stream=tiles_per_stream),
        grid=(n_row * n_col + 1,),
        in_specs=[
            pl.BlockSpec((tm, d), lambda n: (cur(n) // n_col, 0)),
            pl.BlockSpec((d, tn), lambda n: (0, col0 + cur(n) % n_col)),
            pl.BlockSpec((CONV_WIDTH, tn), lambda n: (0, prv(n) % n_col)),
            pl.BlockSpec((1, tn), lambda n: (0, prv(n) % n_col)),
            pl.BlockSpec((1, CONV_PAD_ROWS, tn), lambda n: (stream(n), 0, prv(n) % n_col)),
        ],
        out_specs=[
            pl.BlockSpec((tm, tn), lambda n: (prv(n) // n_col, prv(n) % n_col)),
            pl.BlockSpec((1, CONV_PAD_ROWS, tn), lambda n: (prv(n) // n_col, 0, prv(n) % n_col)),
        ],
        out_shape=[
            jax.ShapeDtypeStruct((rows, n_out), BF16),
            jax.ShapeDtypeStruct((n_row, CONV_PAD_ROWS, n_out), F32),
        ],
        scratch_shapes=[
            pltpu.VMEM((CONV_PAD_ROWS + tm, tn), F32),
            pltpu.VMEM((CONV_PAD_ROWS + tm, tn), F32),
            pltpu.VMEM((n_col, CONV_PAD_ROWS, tn), F32),
        ],
        compiler_params=_cparams("arbitrary"),
        name="proj_conv",
    )(xn, cols.w, conv_w, conv_b, c0_padded)


def _sgu_kernel(*refs, seg_len, emit_v):
    if emit_v:
        (uv_ref, ga_ref, lng_ref, lnb_ref, wm_ref, bias_ref, wa_ref, o_ref, v_ref,
         gated0_ref, gated1_ref) = refs
    else:
        (uv_ref, ga_ref, lng_ref, lnb_ref, wm_ref, bias_ref, wa_ref, o_ref,
         gated0_ref, gated1_ref) = refs
    n = pl.program_id(0)
    tm = uv_ref.shape[0]
    d = o_ref.shape[1]
    gd = d // SGU_GROUPS

    @pl.when(n == 0)
    def _():
        gated1_ref[...] = jnp.zeros_like(gated1_ref)

    row = lax.broadcasted_iota(jnp.int32, (SGU_CHUNK, SGU_CHUNK), 0)
    col = lax.broadcasted_iota(jnp.int32, (SGU_CHUNK, SGU_CHUNK), 1)
    mask = (col <= row) & ((row // seg_len) == (col // seg_len))

    n_chunks = tm // SGU_CHUNK
    pw = d // n_chunks

    def body(prev_ref, next_ref):
        ws = [jnp.where(mask, wm_ref[g], 0.0).astype(BF16) for g in range(SGU_GROUPS)]
        gated_rows = []
        v_rows = []
        outs = []
        for c in range(n_chunks):
            ps = slice(c * pw, (c + 1) * pw)
            ya = _dot(prev_ref[...], wa_ref[:, ps])
            outs.append((ga_ref[:, ps].astype(F32) * ya).astype(o_ref.dtype))
            rs = slice(c * SGU_CHUNK, (c + 1) * SGU_CHUNK)
            v = uv_ref[rs, d:].astype(F32)
            mu = jnp.mean(v, axis=-1, keepdims=True)
            vc = v - mu
            var = jnp.mean(vc * vc, axis=-1, keepdims=True)
            vn = vc * lax.rsqrt(var + EPS) * lng_ref[...] + lnb_ref[...]
            v_rows.append(vn)
            vb = vn.astype(BF16)
            cols = []
            for g in range(SGU_GROUPS):
                cs = slice(g * gd, (g + 1) * gd)
                mixed = _dot(ws[g], vb[:, cs]) + bias_ref[:, cs]
                cols.append((uv_ref[rs, cs].astype(F32) * mixed).astype(BF16))
            gated_rows.append(jnp.concatenate(cols, axis=1))
        o_ref[...] = jnp.concatenate(outs, axis=1)
        for c in range(n_chunks):
            rs = slice(c * SGU_CHUNK, (c + 1) * SGU_CHUNK)
            next_ref[rs, :] = gated_rows[c]
            if emit_v:
                v_ref[rs, :] = v_rows[c]

    _by_parity(n, body, gated0_ref, gated1_ref)


def _sgu(uv, gates, ln_g, ln_b, wm, bias_full, w_a, seg_len, tm, emit_v):
    rows = uv.shape[0]
    d = w_a.shape[0]
    n_row = rows // tm
    cur, prv = _tile_maps(n_row, 1)
    const2 = lambda n: (0, 0)
    out_specs = [pl.BlockSpec((tm, d), lambda n: (prv(n), 0))]
    out_shape = [jax.ShapeDtypeStruct((rows, d), BF16)]
    if emit_v:
        out_specs.append(pl.BlockSpec((tm, d), lambda n: (cur(n), 0)))
        out_shape.append(jax.ShapeDtypeStruct((rows, d), F32))
    return pl.pallas_call(
        functools.partial(_sgu_kernel, seg_len=seg_len, emit_v=emit_v),
        grid=(n_row + 1,),
        in_specs=[
            pl.BlockSpec((tm, 2 * d), lambda n: (cur(n), 0)),
            pl.BlockSpec((tm, d), lambda n: (prv(n), 0)),
            pl.BlockSpec((1, d), const2),
            pl.BlockSpec((1, d), const2),
            pl.BlockSpec((SGU_GROUPS, SGU_CHUNK, SGU_CHUNK), lambda n: (0, 0, 0)),
            pl.BlockSpec((SGU_CHUNK, d), const2),
            pl.BlockSpec((d, d), const2),
        ],
        out_specs=out_specs,
        out_shape=out_shape,
        scratch_shapes=[pltpu.VMEM((tm, d), BF16), pltpu.VMEM((tm, d), BF16)],
        compiler_params=_cparams("arbitrary"),
        name="sgu",
    )(uv, gates, ln_g, ln_b, wm, bias_full, w_a)


def _ssd_kernel(*refs, n_tiles, valid_last, conv_in_kernel, chunks):
    if conv_in_kernel:
        (xbc_ref, dt_ref, z_ref, a_ref, d_ref, ng_ref, h0_ref, cw_ref, cb_ref, c0_ref,
         y_ref, hout_ref, cout_ref, ht_ref, cbuf_ref) = refs
    else:
        xbc_ref, dt_ref, z_ref, a_ref, d_ref, ng_ref, h0_ref, y_ref, hout_ref, ht_ref = refs
    t = pl.program_id(1)
    q = SSD_CHUNK
    d_ssm = y_ref.shape[1]
    n_heads = d_ssm // SSM_HEADDIM
    gw = d_ssm // SSM_GROUPS
    pair = 2 * SSM_HEADDIM
    n_bc = SSM_GROUPS * SSM_STATE
    assert pair == LANES and q == SSM_HEADDIM

    @pl.when(t == 0)
    def _init():
        for g in range(SSM_GROUPS):
            ht_ref[g] = h0_ref[0, g].T
        if conv_in_kernel:
            cbuf_ref[0:CONV_PAD_ROWS, :] = jnp.zeros((CONV_PAD_ROWS, cbuf_ref.shape[1]), F32)
            cbuf_ref[CONV_PAD_ROWS - (CONV_WIDTH - 1):CONV_PAD_ROWS, :] = c0_ref[0]

    tq = chunks * q
    n_pairs = gw // pair
    if conv_in_kernel:
        cbuf_ref[CONV_PAD_ROWS:CONV_PAD_ROWS + tq, :] = xbc_ref[...].astype(F32)
        conv = cb_ref[...]
        for k in range(CONV_WIDTH):
            r0 = CONV_PAD_ROWS - (CONV_WIDTH - 1) + k
            conv = conv + cbuf_ref[r0:r0 + tq, :] * cw_ref[k:k + 1, :]

        @pl.when(t == n_tiles - 1)
        def _conv_out():
            r0 = CONV_PAD_ROWS + valid_last - (CONV_WIDTH - 1)
            cout_ref[0] = cbuf_ref[r0:r0 + CONV_WIDTH - 1, :]

        cbuf_ref[0:CONV_PAD_ROWS, :] = cbuf_ref[tq:tq + CONV_PAD_ROWS, :]
        xc_all = _act_silu(conv).astype(BF16)
    else:
        xc_all = xbc_ref[...]

    row = lax.broadcasted_iota(jnp.int32, (q, gw), 0)
    pos = lax.broadcasted_iota(jnp.int32, (q, gw), 1) & (SSM_HEADDIM - 1)
    on_diag = row == pos
    causal = row >= pos
    hrow = lax.broadcasted_iota(jnp.int32, (n_heads, d_ssm), 0)
    hlane = lax.broadcasted_iota(jnp.int32, (n_heads, d_ssm), 1)
    expand = (hrow == (hlane // SSM_HEADDIM)).astype(BF16)
    r2 = lax.broadcasted_iota(jnp.int32, (q, q), 0)
    c2 = lax.broadcasted_iota(jnp.int32, (q, q), 1)
    lower = (c2 <= r2).astype(BF16)
    eye = (lax.broadcasted_iota(jnp.int32, (SSM_STATE, SSM_STATE), 0)
           == lax.broadcasted_iota(jnp.int32, (SSM_STATE, SSM_STATE), 1)).astype(BF16)
    first_head = lax.broadcasted_iota(jnp.int32, (q, pair), 1) < SSM_HEADDIM

    state = []
    for g in range(SSM_GROUPS):
        htg = ht_ref[g]
        state.append([htg[:, j * pair:(j + 1) * pair] for j in range(n_pairs)])
    cums = []
    for c in range(chunks):
        dt = dt_ref[c * q:(c + 1) * q, :]
        da_hi, da_lo = _split_bf16(dt * a_ref[...])
        cum = _split_bf16(_dot(lower, da_hi) + _dot(lower, da_lo))
        cums.append((cum, dt.astype(BF16)))

    def decay_terms(c, g):
        rs = slice(c * q, (c + 1) * q)
        gs = slice(g * gw, (g + 1) * gw)
        ns = slice(d_ssm + g * SSM_STATE, d_ssm + (g + 1) * SSM_STATE)
        (cum_hi, cum_lo), dt_bf = cums[c]
        colb = _dot(cum_hi, expand[:, gs]) + _dot(cum_lo, expand[:, gs])
        rowb = jnp.sum(jnp.where(on_diag, colb, 0.0), axis=0, keepdims=True)
        mw = jnp.where(causal, jnp.exp(jnp.minimum(colb - rowb, 0.0)), 0.0)
        expc = jnp.exp(colb)
        tailw = jnp.exp(jnp.minimum(colb[q - 1:q, :] - rowb, 0.0))
        xs = xc_all[rs, gs].astype(F32)
        xdt = (xs * _dot(dt_bf, expand[:, gs])).astype(BF16)
        bg = xc_all[rs, ns]
        cg = xc_all[rs, n_bc + ns.start:n_bc + ns.stop]
        bb = jnp.concatenate([bg, bg], axis=0)
        cb2 = _dot_nt(cg, bb)
        bt2 = _dot_nt(eye, bb)
        return mw, expc, tailw, xs, xdt, cg, cb2, bt2

    def apply_state(c, g, terms):
        mw, expc, tailw, xs, xdt, cg, cb2, bt2 = terms
        rs = slice(c * q, (c + 1) * q)
        gs = slice(g * gw, (g + 1) * gw)
        decl = expc[q - 1:q, :]
        yst = _dot(cg, jnp.concatenate(state[g], axis=1).astype(BF16))
        ys = []
        for j in range(n_pairs):
            js = slice(j * pair, (j + 1) * pair)
            m2 = (cb2 * mw[:, js]).astype(BF16)
            xd2 = xdt[:, js]
            zero = jnp.zeros_like(xd2)
            xbd = jnp.concatenate([jnp.where(first_head, xd2, zero),
                                   jnp.where(first_head, zero, xd2)], axis=0)
            ys.append(_dot(m2, xbd) + expc[:, js] * yst[:, js]
                      + d_ref[:, g * gw + j * pair:g * gw + (j + 1) * pair] * xs[:, js])
            btw = (bt2 * tailw[:, js]).astype(BF16)
            state[g][j] = state[g][j] * decl[:, js] + _dot(btw, xbd)
        yg = jnp.concatenate(ys, axis=1) * z_ref[rs, gs].astype(F32)
        return _rms(yg, ng_ref[:, gs]).astype(y_ref.dtype)

    items = [(c, g) for c in range(chunks) for g in range(SSM_GROUPS)]
    y_out = [[None] * SSM_GROUPS for _ in range(chunks)]
    ahead = [decay_terms(*item) for item in items[:SSD_TERMS_AHEAD]]
    for k, (c, g) in enumerate(items):
        if k + SSD_TERMS_AHEAD < len(items):
            ahead.append(decay_terms(*items[k + SSD_TERMS_AHEAD]))
        y_out[c][g] = apply_state(c, g, ahead.pop(0))

    for c in range(chunks):
        for g in range(SSM_GROUPS):
            y_ref[c * q:(c + 1) * q, g * gw:(g + 1) * gw] = y_out[c][g]
    for g in range(SSM_GROUPS):
        ht_ref[g] = jnp.concatenate(state[g], axis=1)

    @pl.when(t == n_tiles - 1)
    def _state_out():
        for g in range(SSM_GROUPS):
            hout_ref[0, g] = ht_ref[g].T


def _ssd(xbc, dt, zs, a_row, d_full, norm_g, h0, n_streams, n_tiles, valid_last, chunks,
         conv=None):
    rows, conv_dim = xbc.shape
    d_ssm = zs.shape[1]
    n_heads = dt.shape[1]
    gw = d_ssm // SSM_GROUPS
    q = chunks * SSD_CHUNK
    rb = lambda s, t: (s * n_tiles + t, 0)
    const2 = lambda s, t: (0, 0)
    per_stream3 = lambda s, t: (s, 0, 0)
    state_spec = pl.BlockSpec((1, SSM_GROUPS, gw, SSM_STATE), lambda s, t: (s, 0, 0, 0))
    in_specs = [
        pl.BlockSpec((q, conv_dim), rb),
        pl.BlockSpec((q, n_heads), rb),
        pl.BlockSpec((q, d_ssm), rb),
        pl.BlockSpec((1, n_heads), const2),
        pl.BlockSpec((1, d_ssm), const2),
        pl.BlockSpec((1, d_ssm), const2),
        state_spec,
    ]
    args = [xbc, dt, zs, a_row, d_full, norm_g, h0]
    out_specs = [pl.BlockSpec((q, d_ssm), rb), state_spec]
    out_shape = [
        jax.ShapeDtypeStruct((rows, d_ssm), BF16),
        jax.ShapeDtypeStruct((n_streams, SSM_GROUPS, gw, SSM_STATE), F32),
    ]
    scratch = [pltpu.VMEM((SSM_GROUPS, SSM_STATE, gw), F32)]
    if conv is not None:
        in_specs += [
            pl.BlockSpec((CONV_WIDTH, conv_dim), const2),
            pl.BlockSpec((1, conv_dim), const2),
            pl.BlockSpec((1, CONV_WIDTH - 1, conv_dim), per_stream3),
        ]
        args += list(conv)
        out_specs.append(pl.BlockSpec((1, CONV_WIDTH - 1, conv_dim), per_stream3))
        out_shape.append(jax.ShapeDtypeStruct((n_streams, CONV_WIDTH - 1, conv_dim), F32))
        scratch.append(pltpu.VMEM((CONV_PAD_ROWS + q, conv_dim), F32))
    return pl.pallas_call(
        functools.partial(_ssd_kernel, n_tiles=n_tiles, valid_last=valid_last,
                          conv_in_kernel=conv is not None, chunks=chunks),
        grid=(n_streams, n_tiles),
        in_specs=in_specs,
        out_specs=out_specs,
        out_shape=out_shape,
        scratch_shapes=scratch,
        compiler_params=_cparams("arbitrary", "arbitrary"),
        name="ssd",
    )(*args)


def _outproj_kernel(yn_ref, gaya_ref, gb_ref, x_ref, wb_ref, wo_ref, gn_ref, o_ref, hf_ref,
                    keep0_ref, keep1_ref):
    n = pl.program_id(0)
    tm = o_ref.shape[0]
    ec = min(tm, PROJ_EPILOGUE_ROWS)

    @pl.when(n == 0)
    def _():
        keep1_ref[...] = jnp.zeros_like(keep1_ref)

    def body(prev_ref, next_ref):
        normed = [_rms(prev_ref[r:r + ec, :], gn_ref[...]).astype(hf_ref.dtype)
                  for r in range(0, tm, ec)]
        yb = _dot(yn_ref[...], wb_ref[...])
        merged = gaya_ref[...].astype(F32) + gb_ref[...].astype(F32) * yb
        x1 = x_ref[...] + _dot(merged.astype(BF16), wo_ref[...])
        for i, r in enumerate(range(0, tm, ec)):
            hf_ref[r:r + ec, :] = normed[i]
        o_ref[...] = x1
        next_ref[...] = x1

    _by_parity(n, body, keep0_ref, keep1_ref)


def _outproj(yn, gaya, gates, x, w_b, w_o, g_next, tm):
    rows, d = x.shape
    d_ssm = yn.shape[1]
    n_row = rows // tm
    cur, prv = _tile_maps(n_row, 1)
    const2 = lambda n: (0, 0)
    return pl.pallas_call(
        _outproj_kernel,
        grid=(n_row + 1,),
        in_specs=[
            pl.BlockSpec((tm, d_ssm), lambda n: (cur(n), 0)),
            pl.BlockSpec((tm, d), lambda n: (cur(n), 0)),
            pl.BlockSpec((tm, d), lambda n: (cur(n), 1)),
            pl.BlockSpec((tm, d), lambda n: (cur(n), 0)),
            pl.BlockSpec((d_ssm, d), const2, pipeline_mode=pl.Buffered(1)),
            pl.BlockSpec((d, d), const2, pipeline_mode=pl.Buffered(1)),
            pl.BlockSpec((1, d), const2),
        ],
        out_specs=[
            pl.BlockSpec((tm, d), lambda n: (cur(n), 0)),
            pl.BlockSpec((tm, d), lambda n: (prv(n), 0)),
        ],
        out_shape=[
            jax.ShapeDtypeStruct((rows, d), F32),
            jax.ShapeDtypeStruct((rows, d), BF16),
        ],
        scratch_shapes=[pltpu.VMEM((tm, d), F32), pltpu.VMEM((tm, d), F32)],
        compiler_params=_cparams("arbitrary"),
        name="outproj",
    )(yn, gaya, gates, x, w_b, w_o, g_next)


def _ffn_kernel(x_ref, hf_ref, wg_ref, wu_ref, wd_ref, gf_ref, o_ref, acc_ref, *, n_ff):
    j = pl.program_id(1)

    @pl.when(j == 0)
    def _():
        acc_ref[...] = jnp.zeros_like(acc_ref)

    hf = hf_ref[...]
    gate = _dot(hf, wg_ref[...])
    up = _dot(hf, wu_ref[...])
    mid = (_act_silu(gate) * up).astype(BF16)
    acc_ref[...] += _dot(mid, wd_ref[...])

    @pl.when(j == n_ff - 1)
    def _():
        o_ref[...] = _rms(x_ref[...] + acc_ref[...], gf_ref[...])


def _ffn(x, hf, w_gate, w_up, w_down, g_final, tm, tf):
    rows, d = x.shape
    d_ff = w_gate.shape[1]
    n_ff = d_ff // tf
    return pl.pallas_call(
        functools.partial(_ffn_kernel, n_ff=n_ff),
        grid=(rows // tm, n_ff),
        in_specs=[
            pl.BlockSpec((tm, d), lambda i, j: (i, 0)),
            pl.BlockSpec((tm, d), lambda i, j: (i, 0)),
            pl.BlockSpec((d, tf), lambda i, j: (0, j)),
            pl.BlockSpec((d, tf), lambda i, j: (0, j)),
            pl.BlockSpec((tf, d), lambda i, j: (j, 0)),
            pl.BlockSpec((1, d), lambda i, j: (0, 0)),
        ],
        out_specs=pl.BlockSpec((tm, d), lambda i, j: (i, 0)),
        out_shape=jax.ShapeDtypeStruct((rows, d), F32),
        scratch_shapes=[pltpu.VMEM((tm, d), F32)],
        compiler_params=_cparams("arbitrary", "arbitrary"),
        name="ffn",
    )(x, hf, w_gate, w_up, w_down, g_final)


def _pick_tile(rows, preferred):
    tile = min(rows, preferred)
    assert rows % tile == 0, (rows, tile)
    return tile


def _layer(x, h0, c0, p, g_final, want_v):
    n_streams, length, d = x.shape
    rows = n_streams * length
    x2 = x.reshape(rows, d)
    d_ssm = p["w_b"].shape[0]
    n_heads = p["dt_bias"].shape[0]
    conv_dim = p["conv_w"].shape[1]
    tm = _pick_tile(rows, 1024)
    tn = 1024
    uv, xn = _proj(x2, p["w_uv"], None, _act_gelu, BF16, tm, tn,
                   norm_gain=p["norm_mix_g"].reshape(1, d))
    zs = _proj(xn, p["w_z"], None, _act_silu, BF16, tm, tn)
    dt = _proj(xn, p["w_dt"], p["dt_bias"].reshape(1, n_heads), _act_softplus, F32, tm, n_heads)
    gates = _proj(xn, p["w_gates"], None, _sigmoid, BF16, tm, tn)
    conv_b = p["conv_b"].reshape(1, conv_dim)
    fuse_conv = length % tm == 0
    if fuse_conv:
        c0_padded = jnp.pad(c0, ((0, 0), (CONV_PAD_ROWS - (CONV_WIDTH - 1), 0), (0, 0)))
        xbc, conv_tail = _proj_conv(xn, p["w_xbc"], p["conv_w"], conv_b, c0_padded, n_streams,
                                    tm, tn)
    else:
        xbc = _proj(xn, p["w_xbc"], None, _act_none, BF16, tm, tn)

    seg_len = min(length, SGU_CHUNK)
    assert SGU_CHUNK % seg_len == 0 and length % seg_len == 0
    rep = SGU_CHUNK // seg_len
    wm = jnp.tile(p["sgu_w"][:, :seg_len, :seg_len], (1, rep, rep))
    bias_full = jnp.repeat(jnp.tile(p["sgu_b"][:, :seg_len], (1, rep)).T, d // SGU_GROUPS, axis=1)
    sgu_out = _sgu(uv, gates, p["sgu_ln_g"].reshape(1, d), p["sgu_ln_b"].reshape(1, d),
                   wm, bias_full, p["w_a"], seg_len, _pick_tile(rows, 512), want_v)
    gaya = sgu_out[0]
    v_norm = sgu_out[1].reshape(n_streams, length, d) if want_v else None

    chunks = SSD_CHUNKS_PER_STEP if length % (SSD_CHUNKS_PER_STEP * SSD_CHUNK) == 0 else 1
    q = chunks * SSD_CHUNK
    n_tiles = -(-length // q)
    padded = n_tiles * q
    valid_last = length - (n_tiles - 1) * q

    def pad_rows(arr):
        if padded == length:
            return arr
        arr = arr.reshape(n_streams, length, arr.shape[-1])
        arr = jnp.pad(arr, ((0, 0), (0, padded - length), (0, 0)))
        return arr.reshape(n_streams * padded, arr.shape[-1])

    gw = d_ssm // SSM_GROUPS
    a_row = (-jnp.exp(p["a_log"].astype(F32))).reshape(1, n_heads)
    d_full = jnp.repeat(p["d_skip"].astype(F32), SSM_HEADDIM).reshape(1, d_ssm)
    ssd = functools.partial(_ssd, pad_rows(xbc), pad_rows(dt), pad_rows(zs), a_row, d_full,
                            p["ssm_norm_g"].reshape(1, d_ssm),
                            h0.reshape(n_streams, SSM_GROUPS, gw, SSM_STATE),
                            n_streams, n_tiles, valid_last, chunks)
    if fuse_conv:
        yn, h_new = ssd()
        tiles_per_stream = length // tm
        c_new = conv_tail[tiles_per_stream - 1::tiles_per_stream,
                          CONV_PAD_ROWS - (CONV_WIDTH - 1):, :]
    else:
        yn, h_new, c_new = ssd(conv=(p["conv_w"], conv_b, c0))
    if padded != length:
        yn = yn.reshape(n_streams, padded, d_ssm)[:, :length].reshape(rows, d_ssm)
    h_new = h_new.reshape(n_streams, n_heads, SSM_HEADDIM, SSM_STATE)

    x1, hf = _outproj(yn, gaya, gates, x2, p["w_b"], p["w_o"], p["norm_ffn_g"].reshape(1, d),
                      _pick_tile(rows, 256))
    y = _ffn(x1, hf, p["w_gate"], p["w_up"], p["w_down"], g_final.reshape(1, d),
             _pick_tile(rows, 512), 512)
    return y.reshape(n_streams, length, d), h_new, c_new, v_norm


def _layer_params(i, norm_mix_g, w_in, sgu_ln_g, sgu_ln_b, sgu_w, sgu_b, w_a, conv_w, conv_b,
                  dt_bias, a_log, d_skip, ssm_norm_g, w_b, w_o, norm_ffn_g, w_gate, w_up, w_down):
    d = w_in.shape[1]
    d_ssm = w_b.shape[1]
    conv_dim = conv_w.shape[2]
    n_heads = dt_bias.shape[1]
    s0 = 2 * d
    s1 = s0 + d_ssm
    s2 = s1 + conv_dim
    s3 = s2 + n_heads
    wi = w_in[i].astype(BF16)
    return dict(
        norm_mix_g=norm_mix_g[i],
        w_uv=_Cols(wi, 0, s0), w_z=_Cols(wi, s0, d_ssm), w_xbc=_Cols(wi, s1, conv_dim),
        w_dt=_Cols(wi[:, s2:s3], 0, n_heads), w_gates=_Cols(wi[:, s3:], 0, 2 * d),
        sgu_ln_g=sgu_ln_g[i], sgu_ln_b=sgu_ln_b[i], sgu_w=sgu_w[i], sgu_b=sgu_b[i],
        w_a=w_a[i].astype(BF16), conv_w=conv_w[i], conv_b=conv_b[i], dt_bias=dt_bias[i],
        a_log=a_log[i], d_skip=d_skip[i], ssm_norm_g=ssm_norm_g[i],
        w_b=w_b[i].astype(BF16), w_o=w_o[i].astype(BF16), norm_ffn_g=norm_ffn_g[i],
        w_gate=w_gate[i].astype(BF16), w_up=w_up[i].astype(BF16), w_down=w_down[i].astype(BF16),
    )


def kernel(x_prompt, x_sample, state_ssm, state_conv, norm_mix_g, w_in, sgu_ln_g, sgu_ln_b, sgu_w, sgu_b, w_a, conv_w, conv_b, dt_bias, a_log, d_skip, ssm_norm_g, w_b, w_o, norm_ffn_g, w_gate, w_up, w_down, norm_final_g):
    depth = w_in.shape[0]
    assert depth == 1, "multi-layer stacks need the final norm split from the FFN kernel"
    n_prompt = x_prompt.shape[0]
    n_heads = dt_bias.shape[1]
    conv_dim = conv_w.shape[2]
    p = _layer_params(0, norm_mix_g, w_in, sgu_ln_g, sgu_ln_b, sgu_w, sgu_b, w_a, conv_w, conv_b,
                      dt_bias, a_log, d_skip, ssm_norm_g, w_b, w_o, norm_ffn_g, w_gate, w_up,
                      w_down)
    h0_p = jnp.zeros((n_prompt, n_heads, SSM_HEADDIM, SSM_STATE), F32)
    c0_p = jnp.zeros((n_prompt, CONV_WIDTH - 1, conv_dim), F32)
    y_p, h_p, c_p, _ = _layer(x_prompt, h0_p, c0_p, p, norm_final_g, want_v=False)
    y_s, h_s, c_s, v_s = _layer(x_sample, state_ssm[0], state_conv[0], p, norm_final_g,
                                want_v=True)
    return (y_p, y_s, h_p[None], c_p[None], h_s[None], c_s[None], v_s[None])
```

```python
import functools
from typing import NamedTuple

import jax
import jax.numpy as jnp
from jax import lax
from jax.experimental import pallas as pl
from jax.experimental.pallas import tpu as pltpu

F32 = jnp.float32
BF16 = jnp.bfloat16

EPS = 1e-6
SGU_CHUNK = 128
SGU_GROUPS = 8
SSM_HEADDIM = 64
SSM_GROUPS = 8
SSM_STATE = 128
CONV_WIDTH = 4
SSD_CHUNK = 64
SSD_CHUNKS_PER_STEP = 8
SSD_TERMS_AHEAD = 1

V7X_VMEM_LIMIT_BYTES = 56 * 1024 * 1024
LANES = 128
CONV_PAD_ROWS = 8
PROJ_EPILOGUE_ROWS = 32
CONV_EPILOGUE_ROWS = 32

PROJ_ROWS = 1024
PROJ_COLS = 1024
SGU_ROWS = 512
OUTPROJ_ROWS = 256
FFN_ROWS = 512
FFN_COLS = 512


def _cparams(*sem):
    return pltpu.CompilerParams(dimension_semantics=sem, vmem_limit_bytes=V7X_VMEM_LIMIT_BYTES)


def _dot(a, b):
    return jnp.dot(a, b, preferred_element_type=F32)


def _dot_nt(a, b):
    return lax.dot_general(a, b, (((1,), (1,)), ((), ())), preferred_element_type=F32)


def _rms(x, g):
    return x * lax.rsqrt(jnp.mean(x * x, axis=-1, keepdims=True) + EPS) * g


def _split_bf16(x):
    hi = x.astype(BF16)
    lo = (x - hi.astype(F32)).astype(BF16)
    return hi, lo


def _sigmoid(x):
    return 0.5 * jnp.tanh(0.5 * x) + 0.5


def _act_gelu(x):
    return jax.nn.gelu(x, approximate=True)


def _act_silu(x):
    h = 0.5 * x
    return h + h * jnp.tanh(h)


def _act_softplus(x):
    return jnp.maximum(x, 0.0) + jnp.log1p(jnp.exp(-jnp.abs(x)))


def _act_none(x):
    return x


def _rmsnorm_kernel(x_ref, g_ref, o_ref):
    o_ref[...] = _rms(x_ref[...], g_ref[...]).astype(o_ref.dtype)


def _rmsnorm_cast(x, g, tm):
    rows, d = x.shape
    return pl.pallas_call(
        _rmsnorm_kernel,
        grid=(rows // tm,),
        in_specs=[pl.BlockSpec((tm, d), lambda i: (i, 0)), pl.BlockSpec((1, d), lambda i: (0, 0))],
        out_specs=pl.BlockSpec((tm, d), lambda i: (i, 0)),
        out_shape=jax.ShapeDtypeStruct((rows, d), BF16),
        compiler_params=_cparams("arbitrary"),
        name="rmsnorm",
    )(x, g)


def _by_parity(n, body, acc0_ref, acc1_ref):
    @pl.when(n % 2 == 0)
    def _():
        body(acc1_ref, acc0_ref)

    @pl.when(n % 2 == 1)
    def _():
        body(acc0_ref, acc1_ref)


def _proj_kernel(*refs, act, has_bias):
    if has_bias:
        xn_ref, w_ref, b_ref, o_ref, acc0_ref, acc1_ref = refs
    else:
        xn_ref, w_ref, o_ref, acc0_ref, acc1_ref = refs
    n = pl.program_id(0)

    @pl.when(n == 0)
    def _():
        acc1_ref[...] = jnp.zeros_like(acc1_ref)

    tm = o_ref.shape[0]
    ec = min(tm, PROJ_EPILOGUE_ROWS)

    def body(prev_ref, next_ref):
        outs = []
        for r in range(0, tm, ec):
            prev = prev_ref[r:r + ec, :]
            if has_bias:
                prev = prev + b_ref[...]
            outs.append(act(prev).astype(o_ref.dtype))
        res = _dot(xn_ref[...], w_ref[...])
        for i, r in enumerate(range(0, tm, ec)):
            o_ref[r:r + ec, :] = outs[i]
        next_ref[...] = res

    _by_parity(n, body, acc0_ref, acc1_ref)


def _tile_maps(n_row, n_col):
    last = n_row * n_col - 1
    cur = lambda n: jnp.minimum(n, last)
    prv = lambda n: jnp.maximum(n - 1, 0)
    return cur, prv


class _Cols(NamedTuple):
    w: jax.Array
    start: int
    width: int

    def first_block(self, tn):
        assert self.start % tn == 0 and self.width % tn == 0, (self.start, self.width, tn)
        return self.start // tn


def _proj(xn, cols, bias, act, out_dtype, tm, tn):
    rows, d = xn.shape
    n_out = cols.width
    col0 = cols.first_block(tn)
    n_row, n_col = rows // tm, n_out // tn
    cur, prv = _tile_maps(n_row, n_col)
    in_specs = [
        pl.BlockSpec((tm, d), lambda n: (cur(n) // n_col, 0)),
        pl.BlockSpec((d, tn), lambda n: (0, col0 + cur(n) % n_col)),
    ]
    args = [xn, cols.w]
    if bias is not None:
        in_specs.append(pl.BlockSpec((1, tn), lambda n: (0, prv(n) % n_col)))
        args.append(bias)
    return pl.pallas_call(
        functools.partial(_proj_kernel, act=act, has_bias=bias is not None),
        grid=(n_row * n_col + 1,),
        in_specs=in_specs,
        out_specs=pl.BlockSpec((tm, tn), lambda n: (prv(n) // n_col, prv(n) % n_col)),
        out_shape=jax.ShapeDtypeStruct((rows, n_out), out_dtype),
        scratch_shapes=[pltpu.VMEM((tm, tn), F32), pltpu.VMEM((tm, tn), F32)],
        compiler_params=_cparams("arbitrary"),
        name="proj",
    )(*args)


def _proj_conv_kernel(xn_ref, w_ref, cw_ref, cb_ref, c0_ref, o_ref, tail_ref, acc0_ref, acc1_ref,
                      hist_ref, *, n_col, tiles_per_stream):
    n = pl.program_id(0)
    tm = o_ref.shape[0]
    pad = CONV_PAD_ROWS

    @pl.when(n == 0)
    def _():
        acc1_ref[...] = jnp.zeros_like(acc1_ref)
        hist_ref[...] = jnp.zeros_like(hist_ref)

    prv = jnp.maximum(n - 1, 0)
    jp = prv % n_col
    stream_start = ((prv // n_col) % tiles_per_stream) == 0

    ec = min(tm, CONV_EPILOGUE_ROWS)

    def body(prev_ref, next_ref):
        prev_ref[0:pad, :] = jnp.where(stream_start, c0_ref[0], hist_ref[jp])
        last = prev_ref[tm:tm + pad, :]
        hist_ref[jp] = last
        tail_ref[0] = last
        outs = []
        for r in range(0, tm, ec):
            raw = prev_ref[r:r + pad + ec, :]
            conv = raw * cw_ref[CONV_WIDTH - 1:CONV_WIDTH, :] + cb_ref[...]
            for back in range(1, CONV_WIDTH):
                k = CONV_WIDTH - 1 - back
                conv = conv + pltpu.roll(raw, back, 0) * cw_ref[k:k + 1, :]
            outs.append(_act_silu(conv[pad:pad + ec, :]).astype(o_ref.dtype))
        res = _dot(xn_ref[...], w_ref[...])
        for i, r in enumerate(range(0, tm, ec)):
            o_ref[r:r + ec, :] = outs[i]
        next_ref[pad:pad + tm, :] = res

    _by_parity(n, body, acc0_ref, acc1_ref)


def _proj_conv(xn, cols, conv_w, conv_b, c0_padded, n_streams, tm, tn):
    rows, d = xn.shape
    n_out = cols.width
    col0 = cols.first_block(tn)
    n_row, n_col = rows // tm, n_out // tn
    tiles_per_stream = n_row // n_streams
    cur, prv = _tile_maps(n_row, n_col)
    stream = lambda n: prv(n) // n_col // tiles_per_stream
    return pl.pallas_call(
        functools.partial(_proj_conv_kernel, n_col=n_col, tiles_per_stream=tiles_per_stream),
        grid=(n_row * n_col + 1,),
        in_specs=[
            pl.BlockSpec((tm, d), lambda n: (cur(n) // n_col, 0)),
            pl.BlockSpec((d, tn), lambda n: (0, col0 + cur(n) % n_col)),
            pl.BlockSpec((CONV_WIDTH, tn), lambda n: (0, prv(n) % n_col)),
            pl.BlockSpec((1, tn), lambda n: (0, prv(n) % n_col)),
            pl.BlockSpec((1, CONV_PAD_ROWS, tn), lambda n: (stream(n), 0, prv(n) % n_col)),
        ],
        out_specs=[
            pl.BlockSpec((tm, tn), lambda n: (prv(n) // n_col, prv(n) % n_col)),
            pl.BlockSpec((1, CONV_PAD_ROWS, tn), lambda n: (prv(n) // n_col, 0, prv(n) % n_col)),
        ],
        out_shape=[
            jax.ShapeDtypeStruct((rows, n_out), BF16),
            jax.ShapeDtypeStruct((n_row, CONV_PAD_ROWS, n_out), F32),
        ],
        scratch_shapes=[
            pltpu.VMEM((CONV_PAD_ROWS + tm, tn), F32),
            pltpu.VMEM((CONV_PAD_ROWS + tm, tn), F32),
            pltpu.VMEM((n_col, CONV_PAD_ROWS, tn), F32),
        ],
        compiler_params=_cparams("arbitrary"),
        name="proj_conv",
    )(xn, cols.w, conv_w, conv_b, c0_padded)


def _sgu_kernel(*refs, seg_len, emit_v):
    if emit_v:
        (uv_ref, ga_ref, lng_ref, lnb_ref, wm_ref, bias_ref, wa_ref, o_ref, v_ref,
         gated0_ref, gated1_ref) = refs
    else:
        (uv_ref, ga_ref, lng_ref, lnb_ref, wm_ref, bias_ref, wa_ref, o_ref,
         gated0_ref, gated1_ref) = refs
    n = pl.program_id(0)
    tm = uv_ref.shape[0]
    d = o_ref.shape[1]
    gd = d // SGU_GROUPS

    @pl.when(n == 0)
    def _():
        gated1_ref[...] = jnp.zeros_like(gated1_ref)

    row = lax.broadcasted_iota(jnp.int32, (SGU_CHUNK, SGU_CHUNK), 0)
    col = lax.broadcasted_iota(jnp.int32, (SGU_CHUNK, SGU_CHUNK), 1)
    mask = (col <= row) & ((row // seg_len) == (col // seg_len))

    n_chunks = tm // SGU_CHUNK
    pw = d // n_chunks

    def body(prev_ref, next_ref):
        ws = [jnp.where(mask, wm_ref[g], 0.0).astype(BF16) for g in range(SGU_GROUPS)]
        gated_rows = []
        v_rows = []
        outs = []
        for c in range(n_chunks):
            ps = slice(c * pw, (c + 1) * pw)
            ya = _dot(prev_ref[...], wa_ref[:, ps])
            outs.append((ga_ref[:, ps].astype(F32) * ya).astype(o_ref.dtype))
            rs = slice(c * SGU_CHUNK, (c + 1) * SGU_CHUNK)
            v = uv_ref[rs, d:].astype(F32)
            mu = jnp.mean(v, axis=-1, keepdims=True)
            vc = v - mu
            var = jnp.mean(vc * vc, axis=-1, keepdims=True)
            vn = vc * lax.rsqrt(var + EPS) * lng_ref[...] + lnb_ref[...]
            v_rows.append(vn)
            vb = vn.astype(BF16)
            cols = []
            for g in range(SGU_GROUPS):
                cs = slice(g * gd, (g + 1) * gd)
                mixed = _dot(ws[g], vb[:, cs]) + bias_ref[:, cs]
                cols.append((uv_ref[rs, cs].astype(F32) * mixed).astype(BF16))
            gated_rows.append(jnp.concatenate(cols, axis=1))
        o_ref[...] = jnp.concatenate(outs, axis=1)
        for c in range(n_chunks):
            rs = slice(c * SGU_CHUNK, (c + 1) * SGU_CHUNK)
            next_ref[rs, :] = gated_rows[c]
            if emit_v:
                v_ref[rs, :] = v_rows[c]

    _by_parity(n, body, gated0_ref, gated1_ref)


def _sgu(uv, gates, ln_g, ln_b, wm, bias_full, w_a, seg_len, tm, emit_v):
    rows = uv.shape[0]
    d = w_a.shape[0]
    n_row = rows // tm
    cur, prv = _tile_maps(n_row, 1)
    const2 = lambda n: (0, 0)
    out_specs = [pl.BlockSpec((tm, d), lambda n: (prv(n), 0))]
    out_shape = [jax.ShapeDtypeStruct((rows, d), BF16)]
    if emit_v:
        out_specs.append(pl.BlockSpec((tm, d), lambda n: (cur(n), 0)))
        out_shape.append(jax.ShapeDtypeStruct((rows, d), F32))
    return pl.pallas_call(
        functools.partial(_sgu_kernel, seg_len=seg_len, emit_v=emit_v),
        grid=(n_row + 1,),
        in_specs=[
            pl.BlockSpec((tm, 2 * d), lambda n: (cur(n), 0)),
            pl.BlockSpec((tm, d), lambda n: (prv(n), 0)),
            pl.BlockSpec((1, d), const2),
            pl.BlockSpec((1, d), const2),
            pl.BlockSpec((SGU_GROUPS, SGU_CHUNK, SGU_CHUNK), lambda n: (0, 0, 0)),
            pl.BlockSpec((SGU_CHUNK, d), const2),
            pl.BlockSpec((d, d), const2),
        ],
        out_specs=out_specs,
        out_shape=out_shape,
        scratch_shapes=[pltpu.VMEM((tm, d), BF16), pltpu.VMEM((tm, d), BF16)],
        compiler_params=_cparams("arbitrary"),
        name="sgu",
    )(uv, gates, ln_g, ln_b, wm, bias_full, w_a)


def _ssd_kernel(*refs, n_tiles, valid_last, conv_in_kernel, chunks):
    if conv_in_kernel:
        (xbc_ref, dt_ref, z_ref, a_ref, d_ref, ng_ref, h0_ref, cw_ref, cb_ref, c0_ref,
         y_ref, hout_ref, cout_ref, ht_ref, cbuf_ref) = refs
    else:
        xbc_ref, dt_ref, z_ref, a_ref, d_ref, ng_ref, h0_ref, y_ref, hout_ref, ht_ref = refs
    t = pl.program_id(1)
    q = SSD_CHUNK
    d_ssm = y_ref.shape[1]
    n_heads = d_ssm // SSM_HEADDIM
    gw = d_ssm // SSM_GROUPS
    pair = 2 * SSM_HEADDIM
    n_bc = SSM_GROUPS * SSM_STATE
    assert pair == LANES and q == SSM_HEADDIM

    @pl.when(t == 0)
    def _init():
        for g in range(SSM_GROUPS):
            ht_ref[g] = h0_ref[0, g].T
        if conv_in_kernel:
            cbuf_ref[0:CONV_PAD_ROWS, :] = jnp.zeros((CONV_PAD_ROWS, cbuf_ref.shape[1]), F32)
            cbuf_ref[CONV_PAD_ROWS - (CONV_WIDTH - 1):CONV_PAD_ROWS, :] = c0_ref[0]

    tq = chunks * q
    n_pairs = gw // pair
    if conv_in_kernel:
        cbuf_ref[CONV_PAD_ROWS:CONV_PAD_ROWS + tq, :] = xbc_ref[...].astype(F32)
        conv = cb_ref[...]
        for k in range(CONV_WIDTH):
            r0 = CONV_PAD_ROWS - (CONV_WIDTH - 1) + k
            conv = conv + cbuf_ref[r0:r0 + tq, :] * cw_ref[k:k + 1, :]

        @pl.when(t == n_tiles - 1)
        def _conv_out():
            r0 = CONV_PAD_ROWS + valid_last - (CONV_WIDTH - 1)
            cout_ref[0] = cbuf_ref[r0:r0 + CONV_WIDTH - 1, :]

        cbuf_ref[0:CONV_PAD_ROWS, :] = cbuf_ref[tq:tq + CONV_PAD_ROWS, :]
        xc_all = _act_silu(conv).astype(BF16)
    else:
        xc_all = xbc_ref[...]

    row = lax.broadcasted_iota(jnp.int32, (q, gw), 0)
    pos = lax.broadcasted_iota(jnp.int32, (q, gw), 1) & (SSM_HEADDIM - 1)
    on_diag = row == pos
    causal = row >= pos
    hrow = lax.broadcasted_iota(jnp.int32, (n_heads, d_ssm), 0)
    hlane = lax.broadcasted_iota(jnp.int32, (n_heads, d_ssm), 1)
    expand = (hrow == (hlane // SSM_HEADDIM)).astype(BF16)
    r2 = lax.broadcasted_iota(jnp.int32, (q, q), 0)
    c2 = lax.broadcasted_iota(jnp.int32, (q, q), 1)
    lower = (c2 <= r2).astype(BF16)
    eye = (lax.broadcasted_iota(jnp.int32, (SSM_STATE, SSM_STATE), 0)
           == lax.broadcasted_iota(jnp.int32, (SSM_STATE, SSM_STATE), 1)).astype(BF16)
    first_head = lax.broadcasted_iota(jnp.int32, (q, pair), 1) < SSM_HEADDIM

    state = []
    for g in range(SSM_GROUPS):
        htg = ht_ref[g]
        state.append([htg[:, j * pair:(j + 1) * pair] for j in range(n_pairs)])
    cums = []
    for c in range(chunks):
        dt = dt_ref[c * q:(c + 1) * q, :]
        da_hi, da_lo = _split_bf16(dt * a_ref[...])
        cum = _split_bf16(_dot(lower, da_hi) + _dot(lower, da_lo))
        cums.append((cum, dt.astype(BF16)))

    def decay_terms(c, g):
        rs = slice(c * q, (c + 1) * q)
        gs = slice(g * gw, (g + 1) * gw)
        ns = slice(d_ssm + g * SSM_STATE, d_ssm + (g + 1) * SSM_STATE)
        (cum_hi, cum_lo), dt_bf = cums[c]
        colb = _dot(cum_hi, expand[:, gs]) + _dot(cum_lo, expand[:, gs])
        rowb = jnp.sum(jnp.where(on_diag, colb, 0.0), axis=0, keepdims=True)
        mw = jnp.where(causal, jnp.exp(jnp.minimum(colb - rowb, 0.0)), 0.0)
        expc = jnp.exp(colb)
        tailw = jnp.exp(jnp.minimum(colb[q - 1:q, :] - rowb, 0.0))
        xs = xc_all[rs, gs].astype(F32)
        xdt = (xs * _dot(dt_bf, expand[:, gs])).astype(BF16)
        bg = xc_all[rs, ns]
        cg = xc_all[rs, n_bc + ns.start:n_bc + ns.stop]
        bb = jnp.concatenate([bg, bg], axis=0)
        cb2 = _dot_nt(cg, bb)
        bt2 = _dot_nt(eye, bb)
        return mw, expc, tailw, xs, xdt, cg, cb2, bt2

    def apply_state(c, g, terms):
        mw, expc, tailw, xs, xdt, cg, cb2, bt2 = terms
        rs = slice(c * q, (c + 1) * q)
        gs = slice(g * gw, (g + 1) * gw)
        decl = expc[q - 1:q, :]
        yst = _dot(cg, jnp.concatenate(state[g], axis=1).astype(BF16))
        ys = []
        for j in range(n_pairs):
            js = slice(j * pair, (j + 1) * pair)
            m2 = (cb2 * mw[:, js]).astype(BF16)
            xd2 = xdt[:, js]
            zero = jnp.zeros_like(xd2)
            xbd = jnp.concatenate([jnp.where(first_head, xd2, zero),
                                   jnp.where(first_head, zero, xd2)], axis=0)
            ys.append(_dot(m2, xbd) + expc[:, js] * yst[:, js]
                      + d_ref[:, g * gw + j * pair:g * gw + (j + 1) * pair] * xs[:, js])
            btw = (bt2 * tailw[:, js]).astype(BF16)
            state[g][j] = state[g][j] * decl[:, js] + _dot(btw, xbd)
        yg = jnp.concatenate(ys, axis=1) * z_ref[rs, gs].astype(F32)
        return _rms(yg, ng_ref[:, gs]).astype(y_ref.dtype)

    items = [(c, g) for c in range(chunks) for g in range(SSM_GROUPS)]
    y_out = [[None] * SSM_GROUPS for _ in range(chunks)]
    ahead = [decay_terms(*item) for item in items[:SSD_TERMS_AHEAD]]
    for k, (c, g) in enumerate(items):
        if k + SSD_TERMS_AHEAD < len(items):
            ahead.append(decay_terms(*items[k + SSD_TERMS_AHEAD]))
        y_out[c][g] = apply_state(c, g, ahead.pop(0))

    for c in range(chunks):
        for g in range(SSM_GROUPS):
            y_ref[c * q:(c + 1) * q, g * gw:(g + 1) * gw] = y_out[c][g]
    for g in range(SSM_GROUPS):
        ht_ref[g] = jnp.concatenate(state[g], axis=1)

    @pl.when(t == n_tiles - 1)
    def _state_out():
        for g in range(SSM_GROUPS):
            hout_ref[0, g] = ht_ref[g].T


def _ssd(xbc, dt, zs, a_row, d_full, norm_g, h0, n_streams, n_tiles, valid_last, chunks,
         conv=None):
    rows, conv_dim = xbc.shape
    d_ssm = zs.shape[1]
    n_heads = dt.shape[1]
    gw = d_ssm // SSM_GROUPS
    q = chunks * SSD_CHUNK
    rb = lambda s, t: (s * n_tiles + t, 0)
    const2 = lambda s, t: (0, 0)
    per_stream3 = lambda s, t: (s, 0, 0)
    state_spec = pl.BlockSpec((1, SSM_GROUPS, gw, SSM_STATE), lambda s, t: (s, 0, 0, 0))
    in_specs = [
        pl.BlockSpec((q, conv_dim), rb),
        pl.BlockSpec((q, n_heads), rb),
        pl.BlockSpec((q, d_ssm), rb),
        pl.BlockSpec((1, n_heads), const2),
        pl.BlockSpec((1, d_ssm), const2),
        pl.BlockSpec((1, d_ssm), const2),
        state_spec,
    ]
    args = [xbc, dt, zs, a_row, d_full, norm_g, h0]
    out_specs = [pl.BlockSpec((q, d_ssm), rb), state_spec]
    out_shape = [
        jax.ShapeDtypeStruct((rows, d_ssm), BF16),
        jax.ShapeDtypeStruct((n_streams, SSM_GROUPS, gw, SSM_STATE), F32),
    ]
    scratch = [pltpu.VMEM((SSM_GROUPS, SSM_STATE, gw), F32)]
    if conv is not None:
        in_specs += [
            pl.BlockSpec((CONV_WIDTH, conv_dim), const2),
            pl.BlockSpec((1, conv_dim), const2),
            pl.BlockSpec((1, CONV_WIDTH - 1, conv_dim), per_stream3),
        ]
        args += list(conv)
        out_specs.append(pl.BlockSpec((1, CONV_WIDTH - 1, conv_dim), per_stream3))
        out_shape.append(jax.ShapeDtypeStruct((n_streams, CONV_WIDTH - 1, conv_dim), F32))
        scratch.append(pltpu.VMEM((CONV_PAD_ROWS + q, conv_dim), F32))
    return pl.pallas_call(
        functools.partial(_ssd_kernel, n_tiles=n_tiles, valid_last=valid_last,
                          conv_in_kernel=conv is not None, chunks=chunks),
        grid=(n_streams, n_tiles),
        in_specs=in_specs,
        out_specs=out_specs,
        out_shape=out_shape,
        scratch_shapes=scratch,
        compiler_params=_cparams("arbitrary", "arbitrary"),
        name="ssd",
    )(*args)


def _outproj_kernel(yn_ref, gaya_ref, gb_ref, x_ref, wb_ref, wo_ref, o_ref):
    yb = _dot(yn_ref[...], wb_ref[...])
    merged = gaya_ref[...].astype(F32) + gb_ref[...].astype(F32) * yb
    o_ref[...] = x_ref[...] + _dot(merged.astype(BF16), wo_ref[...])


def _outproj(yn, gaya, gates, x, w_b, w_o, tm):
    rows, d = x.shape
    d_ssm = yn.shape[1]
    return pl.pallas_call(
        _outproj_kernel,
        grid=(rows // tm,),
        in_specs=[
            pl.BlockSpec((tm, d_ssm), lambda i: (i, 0)),
            pl.BlockSpec((tm, d), lambda i: (i, 0)),
            pl.BlockSpec((tm, d), lambda i: (i, 1)),
            pl.BlockSpec((tm, d), lambda i: (i, 0)),
            pl.BlockSpec((d_ssm, d), lambda i: (0, 0), pipeline_mode=pl.Buffered(1)),
            pl.BlockSpec((d, d), lambda i: (0, 0), pipeline_mode=pl.Buffered(1)),
        ],
        out_specs=pl.BlockSpec((tm, d), lambda i: (i, 0)),
        out_shape=jax.ShapeDtypeStruct((rows, d), F32),
        compiler_params=_cparams("arbitrary"),
        name="outproj",
    )(yn, gaya, gates, x, w_b, w_o)


def _ffn_kernel(x_ref, g_ref, wg_ref, wu_ref, wd_ref, gf_ref, o_ref, hf_ref, acc_ref, *, n_ff):
    j = pl.program_id(1)

    @pl.when(j == 0)
    def _():
        hf_ref[...] = _rms(x_ref[...], g_ref[...]).astype(BF16)
        acc_ref[...] = jnp.zeros_like(acc_ref)

    hf = hf_ref[...]
    gate = _dot(hf, wg_ref[...])
    up = _dot(hf, wu_ref[...])
    mid = (_act_silu(gate) * up).astype(BF16)
    acc_ref[...] += _dot(mid, wd_ref[...])

    @pl.when(j == n_ff - 1)
    def _():
        o_ref[...] = _rms(x_ref[...] + acc_ref[...], gf_ref[...])


def _ffn(x, g, w_gate, w_up, w_down, g_final, tm, tf):
    rows, d = x.shape
    d_ff = w_gate.shape[1]
    n_ff = d_ff // tf
    return pl.pallas_call(
        functools.partial(_ffn_kernel, n_ff=n_ff),
        grid=(rows // tm, n_ff),
        in_specs=[
            pl.BlockSpec((tm, d), lambda i, j: (i, 0)),
            pl.BlockSpec((1, d), lambda i, j: (0, 0)),
            pl.BlockSpec((d, tf), lambda i, j: (0, j)),
            pl.BlockSpec((d, tf), lambda i, j: (0, j)),
            pl.BlockSpec((tf, d), lambda i, j: (j, 0)),
            pl.BlockSpec((1, d), lambda i, j: (0, 0)),
        ],
        out_specs=pl.BlockSpec((tm, d), lambda i, j: (i, 0)),
        out_shape=jax.ShapeDtypeStruct((rows, d), F32),
        scratch_shapes=[pltpu.VMEM((tm, d), BF16), pltpu.VMEM((tm, d), F32)],
        compiler_params=_cparams("arbitrary", "arbitrary"),
        name="ffn",
    )(x, g, w_gate, w_up, w_down, g_final)


def _pick_tile(rows, preferred):
    tile = min(rows, preferred)
    assert rows % tile == 0, (rows, tile)
    return tile


def _layer(x, h0, c0, p, g_final, want_v):
    n_streams, length, d = x.shape
    rows = n_streams * length
    x2 = x.reshape(rows, d)
    d_ssm = p["w_b"].shape[0]
    n_heads = p["dt_bias"].shape[0]
    conv_dim = p["conv_w"].shape[1]
    tm = _pick_tile(rows, PROJ_ROWS)
    tn = PROJ_COLS
    xn = _rmsnorm_cast(x2, p["norm_mix_g"].reshape(1, d), tm)
    uv = _proj(xn, p["w_uv"], None, _act_gelu, BF16, tm, tn)
    zs = _proj(xn, p["w_z"], None, _act_silu, BF16, tm, tn)
    dt = _proj(xn, p["w_dt"], p["dt_bias"].reshape(1, n_heads), _act_softplus, F32, tm, n_heads)
    gates = _proj(xn, p["w_gates"], None, _sigmoid, BF16, tm, tn)
    conv_b = p["conv_b"].reshape(1, conv_dim)
    fuse_conv = length % tm == 0
    if fuse_conv:
        c0_padded = jnp.pad(c0, ((0, 0), (CONV_PAD_ROWS - (CONV_WIDTH - 1), 0), (0, 0)))
        xbc, conv_tail = _proj_conv(xn, p["w_xbc"], p["conv_w"], conv_b, c0_padded, n_streams,
                                    tm, tn)
    else:
        xbc = _proj(xn, p["w_xbc"], None, _act_none, BF16, tm, tn)

    seg_len = min(length, SGU_CHUNK)
    assert SGU_CHUNK % seg_len == 0 and length % seg_len == 0
    rep = SGU_CHUNK // seg_len
    wm = jnp.tile(p["sgu_w"][:, :seg_len, :seg_len], (1, rep, rep))
    bias_full = jnp.repeat(jnp.tile(p["sgu_b"][:, :seg_len], (1, rep)).T, d // SGU_GROUPS, axis=1)
    sgu_out = _sgu(uv, gates, p["sgu_ln_g"].reshape(1, d), p["sgu_ln_b"].reshape(1, d),
                   wm, bias_full, p["w_a"], seg_len, _pick_tile(rows, SGU_ROWS), want_v)
    gaya = sgu_out[0]
    v_norm = sgu_out[1].reshape(n_streams, length, d) if want_v else None

    chunks = SSD_CHUNKS_PER_STEP if length % (SSD_CHUNKS_PER_STEP * SSD_CHUNK) == 0 else 1
    q = chunks * SSD_CHUNK
    n_tiles = -(-length // q)
    padded = n_tiles * q
    valid_last = length - (n_tiles - 1) * q

    def pad_rows(arr):
        if padded == length:
            return arr
        arr = arr.reshape(n_streams, length, arr.shape[-1])
        arr = jnp.pad(arr, ((0, 0), (0, padded - length), (0, 0)))
        return arr.reshape(n_streams * padded, arr.shape[-1])

    gw = d_ssm // SSM_GROUPS
    a_row = (-jnp.exp(p["a_log"].astype(F32))).reshape(1, n_heads)
    d_full = jnp.repeat(p["d_skip"].astype(F32), SSM_HEADDIM).reshape(1, d_ssm)
    ssd = functools.partial(_ssd, pad_rows(xbc), pad_rows(dt), pad_rows(zs), a_row, d_full,
                            p["ssm_norm_g"].reshape(1, d_ssm),
                            h0.reshape(n_streams, SSM_GROUPS, gw, SSM_STATE),
                            n_streams, n_tiles, valid_last, chunks)
    if fuse_conv:
        yn, h_new = ssd()
        tiles_per_stream = length // tm
        c_new = conv_tail[tiles_per_stream - 1::tiles_per_stream,
                          CONV_PAD_ROWS - (CONV_WIDTH - 1):, :]
    else:
        yn, h_new, c_new = ssd(conv=(p["conv_w"], conv_b, c0))
    if padded != length:
        yn = yn.reshape(n_streams, padded, d_ssm)[:, :length].reshape(rows, d_ssm)
    h_new = h_new.reshape(n_streams, n_heads, SSM_HEADDIM, SSM_STATE)

    x1 = _outproj(yn, gaya, gates, x2, p["w_b"], p["w_o"], _pick_tile(rows, OUTPROJ_ROWS))
    y = _ffn(x1, p["norm_ffn_g"].reshape(1, d), p["w_gate"], p["w_up"], p["w_down"],
             g_final.reshape(1, d), _pick_tile(rows, FFN_ROWS), FFN_COLS)
    return y.reshape(n_streams, length, d), h_new, c_new, v_norm


def _layer_params(i, norm_mix_g, w_in, sgu_ln_g, sgu_ln_b, sgu_w, sgu_b, w_a, conv_w, conv_b,
                  dt_bias, a_log, d_skip, ssm_norm_g, w_b, w_o, norm_ffn_g, w_gate, w_up, w_down):
    d = w_in.shape[1]
    d_ssm = w_b.shape[1]
    conv_dim = conv_w.shape[2]
    n_heads = dt_bias.shape[1]
    s0 = 2 * d
    s1 = s0 + d_ssm
    s2 = s1 + conv_dim
    s3 = s2 + n_heads
    wi = w_in[i].astype(BF16)
    return dict(
        norm_mix_g=norm_mix_g[i],
        w_uv=_Cols(wi, 0, s0), w_z=_Cols(wi, s0, d_ssm), w_xbc=_Cols(wi, s1, conv_dim),
        w_dt=_Cols(wi[:, s2:s3], 0, n_heads), w_gates=_Cols(wi[:, s3:], 0, 2 * d),
        sgu_ln_g=sgu_ln_g[i], sgu_ln_b=sgu_ln_b[i], sgu_w=sgu_w[i], sgu_b=sgu_b[i],
        w_a=w_a[i].astype(BF16), conv_w=conv_w[i], conv_b=conv_b[i], dt_bias=dt_bias[i],
        a_log=a_log[i], d_skip=d_skip[i], ssm_norm_g=ssm_norm_g[i],
        w_b=w_b[i].astype(BF16), w_o=w_o[i].astype(BF16), norm_ffn_g=norm_ffn_g[i],
        w_gate=w_gate[i].astype(BF16), w_up=w_up[i].astype(BF16), w_down=w_down[i].astype(BF16),
    )


def kernel(x_prompt, x_sample, state_ssm, state_conv, norm_mix_g, w_in, sgu_ln_g, sgu_ln_b, sgu_w, sgu_b, w_a, conv_w, conv_b, dt_bias, a_log, d_skip, ssm_norm_g, w_b, w_o, norm_ffn_g, w_gate, w_up, w_down, norm_final_g):
    depth = w_in.shape[0]
    assert depth == 1, "multi-layer stacks need the final norm split from the FFN kernel"
    n_prompt = x_prompt.shape[0]
    n_heads = dt_bias.shape[1]
    conv_dim = conv_w.shape[2]
    p = _layer_params(0, norm_mix_g, w_in, sgu_ln_g, sgu_ln_b, sgu_w, sgu_b, w_a, conv_w, conv_b,
                      dt_bias, a_log, d_skip, ssm_norm_g, w_b, w_o, norm_ffn_g, w_gate, w_up,
                      w_down)
    h0_p = jnp.zeros((n_prompt, n_heads, SSM_HEADDIM, SSM_STATE), F32)
    c0_p = jnp.zeros((n_prompt, CONV_WIDTH - 1, conv_dim), F32)
    y_p, h_p, c_p, _ = _layer(x_prompt, h0_p, c0_p, p, norm_final_g, want_v=False)
    y_s, h_s, c_s, v_s = _layer(x_sample, state_ssm[0], state_conv[0], p, norm_final_g,
                                want_v=True)
    return (y_p, y_s, h_p[None], c_p[None], h_s[None], c_s[None], v_s[None])
```
